```python
import jax, jax.numpy as jnp
from jax import lax
import numpy as np

D_MODEL = 1024
BATCH = 4
SEQ = 8192
DEPTH = 1
DEC_BATCH = 128
DEC_SEQ = 4
PAST_LEN = 16384
PAGE_SIZE = 128

HGRN_HEADS = 4
HGRN_KDIM = 128
HGRN_VDIM = 128
HGRN_CHUNK = 64
N_Q_HEADS = 8
N_KV_HEADS = 2
HEAD_DIM = 64
Q_PER_KV = N_Q_HEADS // N_KV_HEADS
WINDOW = 128
D_FF = -(-(8 * D_MODEL) // (3 * 256)) * 256

HGRN_FWIDTH = HGRN_HEADS * HGRN_KDIM
HGRN_VWIDTH = HGRN_HEADS * HGRN_VDIM
ATTN_QWIDTH = N_Q_HEADS * HEAD_DIM
ATTN_KVWIDTH = N_KV_HEADS * HEAD_DIM
PROJ_SIZES = (HGRN_FWIDTH, HGRN_FWIDTH, HGRN_VWIDTH, HGRN_VWIDTH,
              ATTN_QWIDTH, ATTN_KVWIDTH, ATTN_KVWIDTH, D_MODEL, D_MODEL)
PROJ_WIDTH = sum(PROJ_SIZES)
EPS = 1e-6
NEG_BIG = -1e30

kernel_name = "hgrn2_swa_sink_alibi_parallel_hybrid_step"


def _split_points():
    pts, acc = [], 0
    for s in PROJ_SIZES[:-1]:
        acc += s
        pts.append(acc)
    return pts


def rms_norm(x, w):
    xf = x.astype(jnp.float32)
    xf = xf * lax.rsqrt(jnp.mean(xf * xf, axis=-1, keepdims=True) + EPS)
    return xf * w.astype(jnp.float32)


def alibi_slopes():
    return jnp.exp2(-8.0 * jnp.arange(1, N_Q_HEADS + 1, dtype=jnp.float32) / N_Q_HEADS)


def hgrn2_chunked(q, k, g, v, s0):
    b, t, h, dk = q.shape
    dv = v.shape[-1]
    c = min(HGRN_CHUNK, t)
    n = -(-t // c)
    pad = n * c - t
    if pad:
        pw = ((0, 0), (0, pad), (0, 0), (0, 0))
        q, k, g, v = jnp.pad(q, pw), jnp.pad(k, pw), jnp.pad(g, pw), jnp.pad(v, pw)

    def to_chunks(a):
        return a.reshape(b, n, c, h, a.shape[-1]).transpose(1, 0, 3, 2, 4)

    qc, kc, gc, vc = to_chunks(q), to_chunks(k), to_chunks(g), to_chunks(v)
    causal = jnp.tril(jnp.ones((c, c), dtype=bool))

    def step(s, inp):
        qi, ki, gi, vi = inp
        cum = jnp.cumsum(gi, axis=2)
        o_inter = jnp.einsum('bhtk,bhkv->bhtv', qi * jnp.exp(cum), s)
        diff = cum[:, :, :, None, :] - cum[:, :, None, :, :]
        decay = jnp.exp(jnp.where(causal[:, :, None], diff, -jnp.inf))
        scores = jnp.einsum('bhtk,bhjk,bhtjk->bhtj', qi, ki, decay)
        o = o_inter + jnp.einsum('bhtj,bhjv->bhtv', scores, vi)
        last = cum[:, :, -1]
        s_new = jnp.exp(last)[..., None] * s + jnp.einsum(
            'bhjk,bhjv->bhkv', ki * jnp.exp(last[:, :, None] - cum), vi)
        return s_new, o

    s_fin, o = lax.scan(step, s0, (qc, kc, gc, vc))
    o = o.transpose(1, 0, 3, 2, 4).reshape(b, n * c, h, dv)[:, :t]
    return o, s_fin


def hgrn2_branch(q_raw, f_raw, i_raw, g_raw, s0, lb, hgrn_norm_w):
    b, t, _ = q_raw.shape
    shp = (b, t, HGRN_HEADS, HGRN_KDIM)
    q = (jax.nn.silu(q_raw.astype(jnp.float32)) * (HGRN_KDIM ** -0.5)).reshape(shp)
    lbf = lb.astype(jnp.float32)
    f = lbf + (1.0 - lbf) * jax.nn.sigmoid(f_raw.astype(jnp.float32))
    k = (1.0 - f).reshape(shp)
    g = jnp.log(f).reshape(shp)
    v = i_raw.astype(jnp.float32).reshape(b, t, HGRN_HEADS, HGRN_VDIM)
    o, s_fin = hgrn2_chunked(q, k, g, v, s0.astype(jnp.float32))
    o = rms_norm(o, hgrn_norm_w.reshape(HGRN_HEADS, HGRN_VDIM))
    o = o.reshape(b, t, HGRN_VWIDTH) * jax.nn.silu(g_raw.astype(jnp.float32))
    return o, s_fin


def swa_core(q, k, v, q_pos, k_pos, sinks):
    s = jnp.einsum('bnqhgd,bnkhd->bnhgqk', q, k,
                   preferred_element_type=jnp.float32) * (HEAD_DIM ** -0.5)
    dist = q_pos[:, :, None] - k_pos[:, None, :]
    valid = (dist >= 0) & (dist < WINDOW) & (k_pos[:, None, :] >= 0)
    slopes = alibi_slopes().reshape(N_KV_HEADS, Q_PER_KV)[None, None, :, :, None, None]
    s = s - slopes * dist.astype(jnp.float32)[None, :, None, None]
    s = jnp.where(valid[None, :, None, None], s, NEG_BIG)
    sink = sinks.astype(jnp.float32).reshape(N_KV_HEADS, Q_PER_KV)[None, None, :, :, None, None]
    m = jnp.maximum(jnp.max(s, axis=-1, keepdims=True), sink)
    p = jnp.exp(s - m)
    p = p / (jnp.sum(p, axis=-1, keepdims=True) + jnp.exp(sink - m))
    return jnp.einsum('bnhgqk,bnkhd->bnqhgd', p, v.astype(jnp.float32))


def swa_prompt(q, k, v, sinks):
    b, t = q.shape[:2]
    nb = t // WINDOW
    qb = q.reshape(b, nb, WINDOW, N_KV_HEADS, Q_PER_KV, HEAD_DIM)
    kb = k.reshape(b, nb, WINDOW, N_KV_HEADS, HEAD_DIM)
    vb = v.reshape(b, nb, WINDOW, N_KV_HEADS, HEAD_DIM)
    k_prev = jnp.concatenate([jnp.zeros_like(kb[:, :1]), kb[:, :-1]], axis=1)
    v_prev = jnp.concatenate([jnp.zeros_like(vb[:, :1]), vb[:, :-1]], axis=1)
    k_band = jnp.concatenate([k_prev, kb], axis=2)
    v_band = jnp.concatenate([v_prev, vb], axis=2)
    blk = jnp.arange(nb, dtype=jnp.int32)[:, None] * WINDOW
    q_pos = blk + jnp.arange(WINDOW, dtype=jnp.int32)[None]
    k_pos = blk - WINDOW + jnp.arange(2 * WINDOW, dtype=jnp.int32)[None]
    o = swa_core(qb, k_band, v_band, q_pos, k_pos, sinks)
    return o.reshape(b, t, ATTN_QWIDTH)


def swa_sample(q, k, v, ck, cv, sinks):
    b, t = q.shape[:2]
    cw = ck.shape[1]
    k_all = jnp.concatenate([ck.astype(k.dtype), k], axis=1)
    v_all = jnp.concatenate([cv.astype(v.dtype), v], axis=1)
    q_pos = (PAST_LEN + jnp.arange(t, dtype=jnp.int32))[None]
    k_pos = jnp.concatenate([PAST_LEN - cw + jnp.arange(cw, dtype=jnp.int32), q_pos[0]])[None]
    o = swa_core(q.reshape(b, 1, t, N_KV_HEADS, Q_PER_KV, HEAD_DIM),
                 k_all[:, None], v_all[:, None], q_pos, k_pos, sinks)
    return o.reshape(b, t, ATTN_QWIDTH), k_all[:, -cw:], v_all[:, -cw:]


def layer_forward(x, s0, ck, cv, lb, norm1_w, w_in, hgrn_norm_w, q_norm_w, k_norm_w, sinks,
                  w_o_hgrn, w_o_attn, w_out, norm2_w, w_ffn_gate, w_ffn_up, w_ffn_down):
    b, t, _ = x.shape
    dt = x.dtype
    h = rms_norm(x, norm1_w).astype(dt)
    proj = h @ w_in
    qa, fa, ia, ga, qb, kb, vb, gate_a, gate_b = jnp.split(proj, _split_points(), axis=-1)
    if s0 is None:
        s0 = jnp.zeros((b, HGRN_HEADS, HGRN_KDIM, HGRN_VDIM), jnp.float32)
    y_a, s_new = hgrn2_branch(qa, fa, ia, ga, s0, lb, hgrn_norm_w)
    q = rms_norm(qb.reshape(b, t, N_Q_HEADS, HEAD_DIM), q_norm_w).astype(dt)
    k = rms_norm(kb.reshape(b, t, N_KV_HEADS, HEAD_DIM), k_norm_w).astype(dt)
    v = vb.reshape(b, t, N_KV_HEADS, HEAD_DIM)
    if ck is None:
        y_b = swa_prompt(q, k, v, sinks)
        cw = min(WINDOW, t)
        k_new, v_new = k[:, t - cw:], v[:, t - cw:]
    else:
        y_b, k_new, v_new = swa_sample(q, k, v, ck, cv, sinks)
    branch_a = (y_a.astype(dt) @ w_o_hgrn).astype(jnp.float32)
    branch_b = (y_b.astype(dt) @ w_o_attn).astype(jnp.float32)
    merged = (jax.nn.sigmoid(gate_a.astype(jnp.float32)) * branch_a
              + jax.nn.sigmoid(gate_b.astype(jnp.float32)) * branch_b)
    x = x + merged.astype(dt) @ w_out
    h2 = rms_norm(x, norm2_w).astype(dt)
    x = x + (jax.nn.silu(h2 @ w_ffn_gate) * (h2 @ w_ffn_up)) @ w_ffn_down
    return x, s_new, k_new, v_new


def setup_inputs(seed: int = 0) -> dict:
    key = jax.random.key(seed)
    ks = jax.random.split(key, 20)

    def nrm(k, shape, scale):
        return jax.random.normal(k, shape, jnp.float32) * scale

    cw = min(WINDOW, PAST_LEN)
    return {
        "x_prompt": nrm(ks[0], (BATCH, SEQ, D_MODEL), 1.0),
        "x_sample": nrm(ks[1], (DEC_BATCH, DEC_SEQ, D_MODEL), 1.0),
        "state_hgrn": nrm(ks[2], (DEPTH, DEC_BATCH, HGRN_HEADS, HGRN_KDIM, HGRN_VDIM), 0.5),
        "cache_k": nrm(ks[3], (DEPTH, DEC_BATCH, cw, N_KV_HEADS, HEAD_DIM), 1.0),
        "cache_v": nrm(ks[4], (DEPTH, DEC_BATCH, cw, N_KV_HEADS, HEAD_DIM), 1.0),
        "norm1_w": 1.0 + nrm(ks[5], (DEPTH, D_MODEL), 0.01),
        "w_in": nrm(ks[6], (DEPTH, D_MODEL, PROJ_WIDTH), D_MODEL ** -0.5),
        "lb_logits": nrm(ks[7], (DEPTH + 1, HGRN_FWIDTH), 0.1),
        "hgrn_norm_w": 1.0 + nrm(ks[8], (DEPTH, HGRN_VWIDTH), 0.01),
        "q_norm_w": 1.0 + nrm(ks[9], (DEPTH, HEAD_DIM), 0.01),
        "k_norm_w": 1.0 + nrm(ks[10], (DEPTH, HEAD_DIM), 0.01),
        "sinks": nrm(ks[11], (DEPTH, N_Q_HEADS), 0.5),
        "w_o_hgrn": nrm(ks[12], (DEPTH, HGRN_VWIDTH, D_MODEL), HGRN_VWIDTH ** -0.5),
        "w_o_attn": nrm(ks[13], (DEPTH, ATTN_QWIDTH, D_MODEL), ATTN_QWIDTH ** -0.5),
        "w_out": nrm(ks[14], (DEPTH, D_MODEL, D_MODEL), D_MODEL ** -0.5),
        "norm2_w": 1.0 + nrm(ks[15], (DEPTH, D_MODEL), 0.01),
        "w_ffn_gate": nrm(ks[16], (DEPTH, D_MODEL, D_FF), D_MODEL ** -0.5),
        "w_ffn_up": nrm(ks[17], (DEPTH, D_MODEL, D_FF), D_MODEL ** -0.5),
        "w_ffn_down": nrm(ks[18], (DEPTH, D_FF, D_MODEL), D_FF ** -0.5),
    }


def reference(x_prompt, x_sample, state_hgrn, cache_k, cache_v, norm1_w, w_in, lb_logits,
              hgrn_norm_w, q_norm_w, k_norm_w, sinks, w_o_hgrn, w_o_attn, w_out, norm2_w,
              w_ffn_gate, w_ffn_up, w_ffn_down):
    lower_bounds = jnp.cumsum(jax.nn.softmax(lb_logits.astype(jnp.float32), axis=0), axis=0)
    xp, xs = x_prompt, x_sample
    sp_list, kp_list, vp_list, ss_list, ksl, vsl = [], [], [], [], [], []
    for l in range(DEPTH):
        w = (lower_bounds[l], norm1_w[l], w_in[l], hgrn_norm_w[l], q_norm_w[l], k_norm_w[l],
             sinks[l], w_o_hgrn[l], w_o_attn[l], w_out[l], norm2_w[l], w_ffn_gate[l],
             w_ffn_up[l], w_ffn_down[l])
        xp, sp, kp, vp = layer_forward(xp, None, None, None, *w)
        xs, ss, kss, vss = layer_forward(xs, state_hgrn[l], cache_k[l], cache_v[l], *w)
        sp_list.append(sp)
        kp_list.append(kp)
        vp_list.append(vp)
        ss_list.append(ss)
        ksl.append(kss)
        vsl.append(vss)
    return (xp, xs, jnp.stack(sp_list), jnp.stack(kp_list), jnp.stack(vp_list),
            jnp.stack(ss_list), jnp.stack(ksl), jnp.stack(vsl))
```

```python
import functools

import jax
import jax.numpy as jnp
from jax import lax
from jax.experimental import pallas as pl
from jax.experimental.pallas import tpu as pltpu

F32 = jnp.float32
BF16 = jnp.bfloat16

EPS = 1e-6
NEG_BIG = -1e30

HG_HEADS = 4
HG_DK = 128
HG_DV = 128
HG_W = HG_HEADS * HG_DK
N_Q_HEADS = 8
N_KV_HEADS = 2
Q_PER_KV = N_Q_HEADS // N_KV_HEADS
HEAD_DIM = 64
HEAD_SHIFT = HEAD_DIM.bit_length() - 1
WINDOW = 128
Q_W = N_Q_HEADS * HEAD_DIM
KV_W = N_KV_HEADS * HEAD_DIM
GROUP_W = Q_PER_KV * HEAD_DIM

SEQ_PAD = 8
ROW_TILE = 512
CHUNK = 128
FFN_CHUNK = 256
DIAG = 64
VMEM_LIMIT = 56 * 1024 * 1024


def _nt_dot(a, b):
    return lax.dot_general(a, b, (((1,), (1,)), ((), ())), preferred_element_type=F32)


def _tn_dot(a, b):
    return lax.dot_general(a, b, (((0,), (0,)), ((), ())), preferred_element_type=F32)


def _sigmoid(x):
    return 1.0 / (1.0 + jnp.exp(-x))


def _const_spec(shape):
    nd = len(shape)
    return pl.BlockSpec(shape, lambda *_: (0,) * nd, pipeline_mode=pl.Buffered(1))


def _proj_kernel(x_ref, n1_ref, win_ref, lbl_ref, qn_ref, kn_ref, hsel_ref,
                 hq_ref, f_ref, hv_ref, og_ref, aq_ref, ak_ref, av_ref, ga_ref, gb_ref):
    x = x_ref[...]
    h = x * lax.rsqrt(jnp.mean(x * x, axis=-1, keepdims=True) + EPS) * n1_ref[...]
    hb = h.astype(BF16)

    def proj(lo, hi):
        return jnp.dot(hb, win_ref[:, lo:hi], preferred_element_type=F32)

    c = 0
    qa = proj(c, c + HG_W); c += HG_W
    hq_ref[...] = (qa * _sigmoid(qa) * (HG_DK ** -0.5)).astype(BF16)

    fa = proj(c, c + HG_W); c += HG_W
    lg = lbl_ref[...]
    e = jnp.exp(lg - jnp.max(lg, axis=0, keepdims=True))
    lb = e[0:1, :] / jnp.sum(e, axis=0, keepdims=True)
    f_ref[...] = lb + (1.0 - lb) * _sigmoid(fa)

    hv_ref[...] = proj(c, c + HG_W).astype(BF16); c += HG_W
    go = proj(c, c + HG_W); c += HG_W
    og_ref[...] = (go * _sigmoid(go)).astype(BF16)

    qb = proj(c, c + Q_W); c += Q_W
    msq = jnp.dot((qb * qb).astype(BF16), hsel_ref[...], preferred_element_type=F32)
    aq_ref[...] = (qb * lax.rsqrt(msq + EPS) * qn_ref[...]).astype(BF16)
    kb = proj(c, c + KV_W); c += KV_W
    msk = jnp.dot((kb * kb).astype(BF16), hsel_ref[0:KV_W, 0:KV_W], preferred_element_type=F32)
    ak_ref[...] = kb * lax.rsqrt(msk + EPS) * kn_ref[...]
    av_ref[...] = proj(c, c + KV_W); c += KV_W

    d = x.shape[-1]
    ga_ref[...] = _sigmoid(proj(c, c + d)).astype(BF16); c += d
    gb_ref[...] = _sigmoid(proj(c, c + d)).astype(BF16)


def _proj(x, n1, win, lbl, qn, kn, hsel):
    n, d = x.shape
    pw = win.shape[1]
    tm = ROW_TILE
    row = lambda w: pl.BlockSpec((tm, w), lambda i: (i, 0))
    out_shapes = (
        jax.ShapeDtypeStruct((n, HG_W), BF16),
        jax.ShapeDtypeStruct((n, HG_W), F32),
        jax.ShapeDtypeStruct((n, HG_W), BF16),
        jax.ShapeDtypeStruct((n, HG_W), BF16),
        jax.ShapeDtypeStruct((n, Q_W), BF16),
        jax.ShapeDtypeStruct((n, KV_W), F32),
        jax.ShapeDtypeStruct((n, KV_W), F32),
        jax.ShapeDtypeStruct((n, d), BF16),
        jax.ShapeDtypeStruct((n, d), BF16),
    )
    return pl.pallas_call(
        _proj_kernel,
        grid=(n // tm,),
        in_specs=[row(d), _const_spec((1, d)), _const_spec((d, pw)), _const_spec(lbl.shape),
                  _const_spec((1, Q_W)), _const_spec((1, KV_W)), _const_spec((Q_W, Q_W))],
        out_specs=[row(HG_W), row(HG_W), row(HG_W), row(HG_W), row(Q_W), row(KV_W), row(KV_W),
                   row(d), row(d)],
        out_shape=out_shapes,
        compiler_params=pltpu.CompilerParams(
            dimension_semantics=("arbitrary",), vmem_limit_bytes=VMEM_LIMIT),
        name="proj",
    )(x, n1, win, lbl, qn, kn, hsel)


def _roll_rows(x, d):
    d = d % x.shape[0]
    return x if d == 0 else pltpu.roll(x, d, axis=0)


def _decay_scans(f, nlev):
    rows = f.shape[0]
    t = lax.broadcasted_iota(jnp.int32, f.shape, 0)
    incl, rexc, tot = [f], [jnp.ones_like(f)], f
    for m in range(nlev):
        h = 1 << m
        upper = (t & h) != 0
        below = _roll_rows(tot, h)
        above = _roll_rows(tot, rows - h)
        incl.append(incl[-1] * jnp.where(upper, below, 1.0))
        rexc.append(rexc[-1] * jnp.where(upper, 1.0, above))
        tot = tot * jnp.where(upper, below, above)
    return incl, rexc, tot


def _pair_levels(rows):
    ti = lax.broadcasted_iota(jnp.int32, (rows, rows), 0)
    tj = lax.broadcasted_iota(jnp.int32, (rows, rows), 1)
    lv = 31 - lax.clz(ti ^ tj)
    return jnp.where(ti > tj, lv, jnp.where(ti == tj, DIAG, -1))


def _hgrn_intra(q, k, v, incl, rexc, nscore):
    lvl = _pair_levels(q.shape[0])
    qs = [(q * incl[m]).astype(BF16) for m in range(nscore)]
    ks = [(k * rexc[m]).astype(BF16) for m in range(nscore)]
    qb, kb = q.astype(BF16), k.astype(BF16)
    outs = []
    for hd in range(HG_HEADS):
        sl = slice(hd * HG_DK, (hd + 1) * HG_DK)
        p = jnp.where(lvl == DIAG, _nt_dot(qb[:, sl], kb[:, sl]), 0.0)
        for m in range(nscore):
            p = jnp.where(lvl == m, _nt_dot(qs[m][:, sl], ks[m][:, sl]), p)
        outs.append(jnp.dot(p.astype(BF16), v[:, sl], preferred_element_type=F32))
    return outs


def _hgrn_finish(o, nw, og):
    return o * lax.rsqrt(jnp.mean(o * o, axis=-1, keepdims=True) + EPS) * nw * og


def _hgrn_prompt_kernel(hq_ref, f_ref, hv_ref, og_ref, nw_ref, ya_ref, sfin_ref, st_ref):
    i = pl.program_id(1)
    nlev = CHUNK.bit_length() - 1

    @pl.when(i == 0)
    def _():
        st_ref[...] = jnp.zeros_like(st_ref)

    q = hq_ref[...].astype(F32)
    f = f_ref[...]
    k = 1.0 - f
    v = hv_ref[...]
    og = og_ref[...].astype(F32)
    nw = nw_ref[...]
    incl, rexc, tot = _decay_scans(f, nlev)
    o_intra = _hgrn_intra(q, k, v, incl, rexc, nlev)
    qd = (q * incl[nlev]).astype(BF16)
    kd = (k * rexc[nlev]).astype(BF16)
    for hd in range(HG_HEADS):
        sl = slice(hd * HG_DK, (hd + 1) * HG_DK)
        st = st_ref[hd]
        o = o_intra[hd] + _nt_dot(qd[:, sl], st.astype(BF16))
        st_ref[hd] = st * tot[0:1, sl] + _tn_dot(v[:, sl], kd[:, sl])
        ya_ref[:, sl] = _hgrn_finish(o, nw[:, sl], og[:, sl]).astype(BF16)

    @pl.when(i == pl.num_programs(1) - 1)
    def _():
        for hd in range(HG_HEADS):
            sfin_ref[0, hd] = st_ref[hd].T


def _hgrn_prompt(hq, f, hv, og, nw, batch, seq):
    n = batch * seq
    nc = seq // CHUNK
    blk = pl.BlockSpec((CHUNK, HG_W), lambda b, i: (b * nc + i, 0))
    return pl.pallas_call(
        _hgrn_prompt_kernel,
        grid=(batch, nc),
        in_specs=[blk, blk, blk, blk, _const_spec((1, HG_W))],
        out_specs=[blk, pl.BlockSpec((1, HG_HEADS, HG_DK, HG_DV), lambda b, i: (b, 0, 0, 0))],
        out_shape=(jax.ShapeDtypeStruct((n, HG_W), BF16),
                   jax.ShapeDtypeStruct((batch, HG_HEADS, HG_DK, HG_DV), F32)),
        scratch_shapes=[pltpu.VMEM((HG_HEADS, HG_DV, HG_DK), F32)],
        compiler_params=pltpu.CompilerParams(
            dimension_semantics=("arbitrary", "arbitrary"), vmem_limit_bytes=VMEM_LIMIT),
        name="hgrn_prompt",
    )(hq, f, hv, og, nw)


def _hgrn_sample_kernel(hq_ref, f_ref, hv_ref, og_ref, nw_ref, s_ref, ya_ref, so_ref, *, seq_len):
    nlev = SEQ_PAD.bit_length() - 1
    t = lax.broadcasted_iota(jnp.int32, (CHUNK, HG_W), 0)
    f = jnp.where((t & (SEQ_PAD - 1)) < seq_len, f_ref[...], 1.0)
    q = hq_ref[...].astype(F32)
    k = 1.0 - f
    v = hv_ref[...]
    vf = v.astype(F32)
    og = og_ref[...].astype(F32)
    nw = nw_ref[...]
    incl, rexc, tot = _decay_scans(f, nlev)
    o_intra = _hgrn_intra(q, k, v, incl, rexc, nlev)
    qd = q * incl[nlev]
    kd = k * rexc[nlev]
    for hd in range(HG_HEADS):
        sl = slice(hd * HG_DK, (hd + 1) * HG_DK)
        o_inter = []
        for s in range(CHUNK // SEQ_PAD):
            rows = slice(s * SEQ_PAD, (s + 1) * SEQ_PAD)
            st = s_ref[s, hd]
            o_inter.append(jnp.dot(qd[rows, sl], st, preferred_element_type=F32))
            decay = jnp.broadcast_to(tot[s * SEQ_PAD:s * SEQ_PAD + 1, sl], (HG_DV, HG_DK)).T
            so_ref[s, hd] = st * decay + _tn_dot(kd[rows, sl], vf[rows, sl])
        o = o_intra[hd] + jnp.concatenate(o_inter, axis=0)
        ya_ref[:, sl] = _hgrn_finish(o, nw[:, sl], og[:, sl]).astype(BF16)


def _hgrn_sample(hq, f, hv, og, nw, state, seq_len):
    n = hq.shape[0]
    spc = CHUNK // SEQ_PAD
    blk = pl.BlockSpec((CHUNK, HG_W), lambda i: (i, 0))
    sblk = pl.BlockSpec((spc, HG_HEADS, HG_DK, HG_DV), lambda i: (i, 0, 0, 0))
    return pl.pallas_call(
        functools.partial(_hgrn_sample_kernel, seq_len=seq_len),
        grid=(n // CHUNK,),
        in_specs=[blk, blk, blk, blk, _const_spec((1, HG_W)), sblk],
        out_specs=[blk, sblk],
        out_shape=(jax.ShapeDtypeStruct((n, HG_W), BF16),
                   jax.ShapeDtypeStruct(state.shape, F32)),
        compiler_params=pltpu.CompilerParams(
            dimension_semantics=("arbitrary",), vmem_limit_bytes=VMEM_LIMIT),
        name="hgrn_sample",
    )(hq, f, hv, og, nw, state)


def _alibi_slope(head):
    return 2.0 ** (-8.0 * (head + 1) / N_Q_HEADS)


def _tile_kv_head(x, g):
    lane = lax.broadcasted_iota(jnp.int32, x.shape, 1)
    other = pltpu.roll(x, HEAD_DIM, axis=1)
    mine = (lane >> HEAD_SHIFT) == g
    one = jnp.where(mine, x, other).astype(BF16)
    return jnp.concatenate([one] * (GROUP_W // KV_W), axis=1)


def _attn_group(qg, kt, vt, bias_of, sink_of):
    nk = kt.shape[0]
    lane_head = lax.broadcasted_iota(jnp.int32, kt.shape, 1) >> HEAD_SHIFT
    zero = jnp.zeros_like(kt)
    kbd = jnp.concatenate([jnp.where(lane_head == hh, kt, zero) for hh in range(Q_PER_KV)], axis=0)
    vbd = jnp.concatenate([jnp.where(lane_head == hh, vt, zero) for hh in range(Q_PER_KV)], axis=0)
    s_all = _nt_dot(qg, kbd)
    ps, invs = [], []
    for hh in range(Q_PER_KV):
        s = s_all[:, hh * nk:(hh + 1) * nk] + bias_of(hh)
        sink = sink_of(hh)
        m = jnp.maximum(jnp.max(s, axis=-1, keepdims=True), sink)
        p = jnp.exp(s - m)
        denom = jnp.sum(p, axis=-1, keepdims=True) + jnp.exp(sink - m)
        ps.append(p.astype(BF16))
        invs.append(1.0 / denom)
    o = jnp.dot(jnp.concatenate(ps, axis=1), vbd, preferred_element_type=F32)
    out_head = lax.broadcasted_iota(jnp.int32, o.shape, 1) >> HEAD_SHIFT
    scale = invs[Q_PER_KV - 1]
    for hh in range(Q_PER_KV - 2, -1, -1):
        scale = jnp.where(out_head == hh, invs[hh], scale)
    return o * scale


def _swa_prompt_kernel(sink_ref, q_ref, kp_ref, kc_ref, vp_ref, vc_ref, y_ref, bias_ref):
    b = pl.program_id(0)
    i = pl.program_id(1)

    @pl.when((b == 0) & (i == 0))
    def _():
        tq = lax.broadcasted_iota(jnp.int32, (WINDOW, 2 * WINDOW), 0)
        c = lax.broadcasted_iota(jnp.int32, (WINDOW, 2 * WINDOW), 1)
        dist = WINDOW + tq - c
        valid = (dist >= 0) & (dist < WINDOW)
        for h in range(N_Q_HEADS):
            bias = jnp.where(valid, -_alibi_slope(h) * dist.astype(F32), NEG_BIG)
            bias_ref[0, h] = bias
            bias_ref[1, h] = jnp.where(c < WINDOW, NEG_BIG, bias)

    table = jnp.where(i == 0, 1, 0)
    k2 = jnp.concatenate([kp_ref[...], kc_ref[...]], axis=0)
    v2 = jnp.concatenate([vp_ref[...], vc_ref[...]], axis=0)
    for g in range(N_KV_HEADS):
        kt = _tile_kv_head(k2, g)
        vt = _tile_kv_head(v2, g)
        lanes = slice(g * GROUP_W, (g + 1) * GROUP_W)
        o = _attn_group(q_ref[:, lanes], kt, vt,
                        lambda hh: bias_ref[table, g * Q_PER_KV + hh],
                        lambda hh: sink_ref[g * Q_PER_KV + hh])
        y_ref[:, lanes] = o.astype(BF16)


def _swa_prompt(sinks, aq, ak, av, batch, seq):
    n = batch * seq
    nb = seq // WINDOW
    cur = lambda w: pl.BlockSpec((WINDOW, w), lambda b, i: (b * nb + i, 0))
    prev = lambda w: pl.BlockSpec((WINDOW, w), lambda b, i: (b * nb + jnp.maximum(i - 1, 0), 0))
    return pl.pallas_call(
        _swa_prompt_kernel,
        grid=(batch, nb),
        in_specs=[pl.BlockSpec(memory_space=pltpu.SMEM),
                  cur(Q_W), prev(KV_W), cur(KV_W), prev(KV_W), cur(KV_W)],
        out_specs=cur(Q_W),
        out_shape=jax.ShapeDtypeStruct((n, Q_W), BF16),
        scratch_shapes=[pltpu.VMEM((2, N_Q_HEADS, WINDOW, 2 * WINDOW), F32)],
        compiler_params=pltpu.CompilerParams(
            dimension_semantics=("arbitrary", "arbitrary"), vmem_limit_bytes=VMEM_LIMIT),
        name="swa_prompt",
    )(sinks, aq, ak, ak, av, av)


SWA_SAMPLE_SEQS = 8


def _swa_sample_kernel(sink_ref, q_ref, kn_ref, vn_ref, ck_ref, cv_ref,
                       y_ref, cko_ref, cvo_ref, bias_ref, *, seq_len):
    cw = ck_ref.shape[1]
    nk = 2 * cw

    @pl.when(pl.program_id(0) == 0)
    def _():
        tq = lax.broadcasted_iota(jnp.int32, (SEQ_PAD, nk), 0)
        c = lax.broadcasted_iota(jnp.int32, (SEQ_PAD, nk), 1)
        dist = jnp.where(c < cw, cw + tq - c, tq - (c - cw))
        valid = (dist >= 0) & (dist < WINDOW) & (c < cw + seq_len)
        for h in range(N_Q_HEADS):
            bias_ref[h] = jnp.where(valid, -_alibi_slope(h) * dist.astype(F32), NEG_BIG)

    row = lax.broadcasted_iota(jnp.int32, (cw, KV_W), 0)
    pad = jnp.zeros((cw - SEQ_PAD, KV_W), F32)

    def body(s, carry):
        ck, cv = ck_ref[s], cv_ref[s]
        kn, vn = kn_ref[s], vn_ref[s]
        k_new = jnp.concatenate([kn, pad], axis=0)
        v_new = jnp.concatenate([vn, pad], axis=0)
        k2 = jnp.concatenate([ck, k_new], axis=0)
        v2 = jnp.concatenate([cv, v_new], axis=0)
        q = q_ref[s]
        for g in range(N_KV_HEADS):
            kt = _tile_kv_head(k2, g)
            vt = _tile_kv_head(v2, g)
            lanes = slice(g * GROUP_W, (g + 1) * GROUP_W)
            o = _attn_group(q[:, lanes], kt, vt,
                            lambda hh: bias_ref[g * Q_PER_KV + hh],
                            lambda hh: sink_ref[g * Q_PER_KV + hh])
            y_ref[s, :, lanes] = o.astype(BF16)
        keep = row < cw - seq_len
        cko_ref[s] = jnp.where(keep, _roll_rows(ck, cw - seq_len), _roll_rows(k_new, cw - seq_len))
        cvo_ref[s] = jnp.where(keep, _roll_rows(cv, cw - seq_len), _roll_rows(v_new, cw - seq_len))
        return carry

    lax.fori_loop(0, q_ref.shape[0], body, 0)


def _swa_sample(sinks, aq, ak, av, cache_k, cache_v, seq_len):
    nseq, cw, _ = cache_k.shape
    g = SWA_SAMPLE_SEQS
    blk = lambda r, w: pl.BlockSpec((g, r, w), lambda i: (i, 0, 0))
    return pl.pallas_call(
        functools.partial(_swa_sample_kernel, seq_len=seq_len),
        grid=(nseq // g,),
        in_specs=[pl.BlockSpec(memory_space=pltpu.SMEM),
                  blk(SEQ_PAD, Q_W), blk(SEQ_PAD, KV_W), blk(SEQ_PAD, KV_W),
                  blk(cw, KV_W), blk(cw, KV_W)],
        out_specs=[blk(SEQ_PAD, Q_W), blk(cw, KV_W), blk(cw, KV_W)],
        out_shape=(jax.ShapeDtypeStruct((nseq, SEQ_PAD, Q_W), BF16),
                   jax.ShapeDtypeStruct((nseq, cw, KV_W), F32),
                   jax.ShapeDtypeStruct((nseq, cw, KV_W), F32)),
        scratch_shapes=[pltpu.VMEM((N_Q_HEADS, SEQ_PAD, 2 * cw), F32)],
        compiler_params=pltpu.CompilerParams(
            dimension_semantics=("arbitrary",), vmem_limit_bytes=VMEM_LIMIT),
        name="swa_sample",
    )(sinks, aq, ak, av, cache_k, cache_v)


def _merge_ffn_kernel(x_ref, ya_ref, yb_ref, ga_ref, gb_ref, woh_ref, woa_ref, wout_ref,
                      n2_ref, wg_ref, wu_ref, wd_ref, o_ref):
    branch_a = jnp.dot(ya_ref[...], woh_ref[...], preferred_element_type=F32)
    branch_b = jnp.dot(yb_ref[...], woa_ref[...], preferred_element_type=F32)
    merged = ga_ref[...].astype(F32) * branch_a + gb_ref[...].astype(F32) * branch_b
    x1 = x_ref[...] + jnp.dot(merged.astype(BF16), wout_ref[...], preferred_element_type=F32)
    h2 = (x1 * lax.rsqrt(jnp.mean(x1 * x1, axis=-1, keepdims=True) + EPS) * n2_ref[...]).astype(BF16)
    acc = x1
    d_ff = wg_ref.shape[1]
    for c in range(d_ff // FFN_CHUNK):
        cols = slice(c * FFN_CHUNK, (c + 1) * FFN_CHUNK)
        gate = jnp.dot(h2, wg_ref[:, cols], preferred_element_type=F32)
        up = jnp.dot(h2, wu_ref[:, cols], preferred_element_type=F32)
        act = (gate * _sigmoid(gate) * up).astype(BF16)
        acc = acc + jnp.dot(act, wd_ref[cols, :], preferred_element_type=F32)
    o_ref[...] = acc


def _merge_ffn(x, ya, yb, ga, gb, woh, woa, wout, n2, wg, wu, wd):
    n, d = x.shape
    d_ff = wg.shape[1]
    tm = ROW_TILE
    row = lambda w: pl.BlockSpec((tm, w), lambda i: (i, 0))
    return pl.pallas_call(
        _merge_ffn_kernel,
        grid=(n // tm,),
        in_specs=[row(d), row(HG_W), row(Q_W), row(d), row(d),
                  _const_spec((HG_W, d)), _const_spec((Q_W, d)), _const_spec((d, d)),
                  _const_spec((1, d)), _const_spec((d, d_ff)), _const_spec((d, d_ff)),
                  _const_spec((d_ff, d))],
        out_specs=row(d),
        out_shape=jax.ShapeDtypeStruct((n, d), F32),
        compiler_params=pltpu.CompilerParams(
            dimension_semantics=("arbitrary",), vmem_limit_bytes=VMEM_LIMIT),
        name="merge_ffn",
    )(x, ya, yb, ga, gb, woh, woa, wout, n2, wg, wu, wd)


def kernel(x_prompt, x_sample, state_hgrn, cache_k, cache_v, norm1_w, w_in, lb_logits, hgrn_norm_w,
           q_norm_w, k_norm_w, sinks, w_o_hgrn, w_o_attn, w_out, norm2_w, w_ffn_gate, w_ffn_up,
           w_ffn_down):
    depth = w_in.shape[0]
    assert depth == 1 and lb_logits.shape[0] == 2, "single-layer step only"
    batch, seq, d = x_prompt.shape
    nseq, dec_seq, _ = x_sample.shape
    cw = cache_k.shape[2]
    assert seq % ROW_TILE == 0 and seq % CHUNK == 0 and seq >= WINDOW
    assert dec_seq <= SEQ_PAD and cw == WINDOW and w_ffn_gate.shape[2] % FFN_CHUNK == 0
    assert (nseq * SEQ_PAD) % ROW_TILE == 0 and nseq % SWA_SAMPLE_SEQS == 0

    win = w_in[0].astype(BF16)
    n1 = norm1_w[0][None, :]
    n2 = norm2_w[0][None, :]
    qn = jnp.tile(q_norm_w[0], N_Q_HEADS)[None, :] * (HEAD_DIM ** -0.5)
    kn = jnp.tile(k_norm_w[0], N_KV_HEADS)[None, :]
    nw = hgrn_norm_w[0][None, :]
    head_of = jnp.arange(Q_W, dtype=jnp.int32) // HEAD_DIM
    hsel = jnp.where(head_of[:, None] == head_of[None, :], 1.0 / HEAD_DIM, 0.0).astype(BF16)
    sk = sinks[0].astype(F32)
    woh, woa, wout = w_o_hgrn[0].astype(BF16), w_o_attn[0].astype(BF16), w_out[0].astype(BF16)
    wg, wu, wd = w_ffn_gate[0].astype(BF16), w_ffn_up[0].astype(BF16), w_ffn_down[0].astype(BF16)

    xp = x_prompt.reshape(batch * seq, d)
    hq, f, hv, og, aq, ak, av, ga, gb = _proj(xp, n1, win, lb_logits, qn, kn, hsel)
    ya, s_prompt = _hgrn_prompt(hq, f, hv, og, nw, batch, seq)
    yb = _swa_prompt(sk, aq, ak, av, batch, seq)
    y_prompt = _merge_ffn(xp, ya, yb, ga, gb, woh, woa, wout, n2, wg, wu, wd).reshape(batch, seq, d)
    k_prompt = ak.reshape(batch, seq, N_KV_HEADS, HEAD_DIM)[:, seq - WINDOW:]
    v_prompt = av.reshape(batch, seq, N_KV_HEADS, HEAD_DIM)[:, seq - WINDOW:]

    xs = jnp.pad(x_sample, ((0, 0), (0, SEQ_PAD - dec_seq), (0, 0))).reshape(nseq * SEQ_PAD, d)
    hq, f, hv, og, aq, ak, av, ga, gb = _proj(xs, n1, win, lb_logits, qn, kn, hsel)
    ya, s_sample = _hgrn_sample(hq, f, hv, og, nw, state_hgrn[0], dec_seq)
    yb, k_sample, v_sample = _swa_sample(
        sk, aq.reshape(nseq, SEQ_PAD, Q_W), ak.reshape(nseq, SEQ_PAD, KV_W),
        av.reshape(nseq, SEQ_PAD, KV_W), cache_k[0].reshape(nseq, cw, KV_W),
        cache_v[0].reshape(nseq, cw, KV_W), dec_seq)
    ys = _merge_ffn(xs, ya, yb.reshape(nseq * SEQ_PAD, Q_W), ga, gb, woh, woa, wout, n2, wg, wu, wd)
    y_sample = ys.reshape(nseq, SEQ_PAD, d)[:, :dec_seq]

    kv_shape = (1, nseq, cw, N_KV_HEADS, HEAD_DIM)
    return (y_prompt, y_sample, s_prompt[None], k_prompt[None], v_prompt[None], s_sample[None],
            k_sample.reshape(kv_shape), v_sample.reshape(kv_shape))
```

```python
import functools

import jax
import jax.numpy as jnp
from jax import lax
from jax.experimental import pallas as pl
from jax.experimental.pallas import tpu as pltpu

F32 = jnp.float32
BF16 = jnp.bfloat16

EPS = 1e-6
NEG_BIG = -1e30

HG_HEADS = 4
HG_DK = 128
HG_DV = 128
HG_W = HG_HEADS * HG_DK
N_Q_HEADS = 8
N_KV_HEADS = 2
Q_PER_KV = N_Q_HEADS // N_KV_HEADS
HEAD_DIM = 64
HEAD_SHIFT = HEAD_DIM.bit_length() - 1
WINDOW = 128
Q_W = N_Q_HEADS * HEAD_DIM
KV_W = N_KV_HEADS * HEAD_DIM
GROUP_W = Q_PER_KV * HEAD_DIM

SEQ_PAD = 8
ROW_TILE = 512
CHUNK = 128
FFN_CHUNK = 256
DOWN_TILE = 1024
MIX_SUB = 4
PROJ_TILE = 256
PROJ_PIECES = 37
HGRN_PIECES = MIX_SUB * 12
SWA_PIECES = MIX_SUB * 20
DIAG = 64
VMEM_LIMIT = 60 * 1024 * 1024


def _nt_dot(a, b):
    return lax.dot_general(a, b, (((1,), (1,)), ((), ())), preferred_element_type=F32)


def _tn_dot(a, b):
    return lax.dot_general(a, b, (((0,), (0,)), ((), ())), preferred_element_type=F32)


def _sigmoid(x):
    return 1.0 / (1.0 + jnp.exp(-x))


def _drain(gen):
    try:
        while True:
            next(gen)
    except StopIteration as stop:
        return stop.value


def _interleave(gens, pieces):
    done = [0] * len(gens)
    results = [None] * len(gens)
    live = list(range(len(gens)))
    while live:
        i = min(live, key=lambda g: ((done[g] + 1) / pieces[g], g))
        try:
            next(gens[i])
            done[i] += 1
        except StopIteration as stop:
            results[i] = stop.value
            live.remove(i)
    return results


def _const_spec(shape):
    nd = len(shape)
    return pl.BlockSpec(shape, lambda *_: (0,) * nd, pipeline_mode=pl.Buffered(1))


def _proj_rows(rows, x_ref, n1_ref, win_ref, lbl_ref, qn_ref, kn_ref, hsel_ref,
               hq_ref, f_ref, hv_ref, og_ref, aq_ref, ak_ref, av_ref, ga_ref, gb_ref):
    x = x_ref[rows, :]
    h = x * lax.rsqrt(jnp.mean(x * x, axis=-1, keepdims=True) + EPS) * n1_ref[...]
    hb = h.astype(BF16)
    yield

    w = PROJ_TILE
    col = [0]

    def proj():
        res = jnp.dot(hb, win_ref[:, col[0]:col[0] + w], preferred_element_type=F32)
        col[0] += w
        return res

    lg = lbl_ref[...]
    e = jnp.exp(lg - jnp.max(lg, axis=0, keepdims=True))
    lb = e[0:1, :] / jnp.sum(e, axis=0, keepdims=True)

    for c in range(0, HG_W, w):
        qa = proj()
        yield
        hq_ref[rows, c:c + w] = (qa * _sigmoid(qa) * (HG_DK ** -0.5)).astype(BF16)
        yield
    for c in range(0, HG_W, w):
        fa = proj()
        yield
        f_ref[rows, c:c + w] = lb[:, c:c + w] + (1.0 - lb[:, c:c + w]) * _sigmoid(fa)
        yield
    for c in range(0, HG_W, w):
        hv_ref[rows, c:c + w] = proj().astype(BF16)
        yield
    for c in range(0, HG_W, w):
        go = proj()
        yield
        og_ref[rows, c:c + w] = (go * _sigmoid(go)).astype(BF16)
        yield

    for c in range(0, Q_W, w):
        qb = proj()
        yield
        msq = jnp.dot((qb * qb).astype(BF16), hsel_ref[0:w, 0:w], preferred_element_type=F32)
        aq_ref[rows, c:c + w] = (qb * lax.rsqrt(msq + EPS) * qn_ref[:, c:c + w]).astype(BF16)
        yield
    assert 2 * KV_W == w
    kv = proj()
    yield
    kb = kv[:, 0:KV_W]
    msk = jnp.dot((kb * kb).astype(BF16), hsel_ref[0:KV_W, 0:KV_W], preferred_element_type=F32)
    ak_ref[rows, :] = kb * lax.rsqrt(msk + EPS) * kn_ref[...]
    av_ref[rows, :] = kv[:, KV_W:2 * KV_W]
    yield

    d = x.shape[-1]
    for gate_ref in (ga_ref, gb_ref):
        for c in range(0, d, w):
            g = proj()
            yield
            gate_ref[rows, c:c + w] = _sigmoid(g).astype(BF16)
            yield


def _proj_kernel(*refs):
    half = ROW_TILE // 2
    first = _proj_rows(slice(0, half), *refs)
    second = _proj_rows(slice(half, ROW_TILE), *refs)
    next(first)
    _interleave([first, second], [1, 1])


def _proj(x, n1, win, lbl, qn, kn, hsel):
    n, d = x.shape
    pw = win.shape[1]
    tm = ROW_TILE
    row = lambda w: pl.BlockSpec((tm, w), lambda i: (i, 0))
    out_shapes = (
        jax.ShapeDtypeStruct((n, HG_W), BF16),
        jax.ShapeDtypeStruct((n, HG_W), F32),
        jax.ShapeDtypeStruct((n, HG_W), BF16),
        jax.ShapeDtypeStruct((n, HG_W), BF16),
        jax.ShapeDtypeStruct((n, Q_W), BF16),
        jax.ShapeDtypeStruct((n, KV_W), F32),
        jax.ShapeDtypeStruct((n, KV_W), F32),
        jax.ShapeDtypeStruct((n, d), BF16),
        jax.ShapeDtypeStruct((n, d), BF16),
    )
    return pl.pallas_call(
        _proj_kernel,
        grid=(n // tm,),
        in_specs=[row(d), _const_spec((1, d)), _const_spec((d, pw)), _const_spec(lbl.shape),
                  _const_spec((1, Q_W)), _const_spec((1, KV_W)), _const_spec((Q_W, Q_W))],
        out_specs=[row(HG_W), row(HG_W), row(HG_W), row(HG_W), row(Q_W), row(KV_W), row(KV_W),
                   row(d), row(d)],
        out_shape=out_shapes,
        compiler_params=pltpu.CompilerParams(
            dimension_semantics=("arbitrary",), vmem_limit_bytes=VMEM_LIMIT),
        name="proj",
    )(x, n1, win, lbl, qn, kn, hsel)


def _roll_rows(x, d):
    d = d % x.shape[0]
    return x if d == 0 else pltpu.roll(x, d, axis=0)


SUBLANES = 8


def _slabs(x):
    return [x[r * SUBLANES:(r + 1) * SUBLANES, :] for r in range(x.shape[0] // SUBLANES)]


def _pair_levels(rows):
    ti = lax.broadcasted_iota(jnp.int32, (rows, rows), 0)
    tj = lax.broadcasted_iota(jnp.int32, (rows, rows), 1)
    lv = 31 - lax.clz(ti ^ tj)
    return jnp.where(ti > tj, lv, jnp.where(ti == tj, DIAG, -1))


def _hgrn_levels(qh, fh, lvl_slabs, nlev):
    ns = qh.shape[0] // SUBLANES
    kh = 1.0 - fh
    qs, ks = _slabs(qh), _slabs(kh)
    incl, tot, rexc = _slabs(fh), _slabs(fh), [None] * ns
    t8 = lax.broadcasted_iota(jnp.int32, (SUBLANES, qh.shape[1]), 0)
    zero = jnp.zeros((SUBLANES, qh.shape[1]), F32)
    qb, kb = qh.astype(BF16), kh.astype(BF16)
    diag = _slabs(_nt_dot(qb, kb))
    p = [jnp.where(lvl_slabs[r] == DIAG, diag[r], 0.0) for r in range(ns)]
    for m in range(nlev):
        h = 1 << m
        hs = h // SUBLANES
        upper = [hs == 0 or (r // hs) % 2 == 1 for r in range(ns)]
        lower = [hs == 0 or (r // hs) % 2 == 0 for r in range(ns)]
        a = jnp.concatenate([qs[r] * incl[r] if upper[r] else zero for r in range(ns)], axis=0)
        if m == 0:
            b = kb
        else:
            b = jnp.concatenate([ks[r] * rexc[r] if lower[r] else zero for r in range(ns)],
                                axis=0).astype(BF16)
        sc = _slabs(_nt_dot(a.astype(BF16), b))
        p = [jnp.where(lvl_slabs[r] == m, sc[r], p[r]) if upper[r] else p[r] for r in range(ns)]
        yield
        if hs == 0:
            up = (t8 & h) != 0
            for r in range(ns):
                below = pltpu.roll(tot[r], h, axis=0)
                above = pltpu.roll(tot[r], SUBLANES - h, axis=0)
                incl[r] = incl[r] * jnp.where(up, below, 1.0)
                grow = jnp.where(up, 1.0, above)
                rexc[r] = grow if rexc[r] is None else rexc[r] * grow
                tot[r] = tot[r] * jnp.where(up, below, above)
        else:
            for b0 in range(0, ns, 2 * hs):
                t_lo, t_hi = tot[b0], tot[b0 + hs]
                both = t_lo * t_hi
                for r in range(b0, b0 + hs):
                    rexc[r] = rexc[r] * t_hi
                    tot[r] = both
                for r in range(b0 + hs, b0 + 2 * hs):
                    incl[r] = incl[r] * t_lo
                    tot[r] = both
    return jnp.concatenate(p, axis=0), qs, ks, incl, rexc, tot


def _hgrn_finish(o, nw, og):
    return o * lax.rsqrt(jnp.mean(o * o, axis=-1, keepdims=True) + EPS) * nw * og


def _hgrn_chunk(q, f, v, og, nw, states):
    nlev = CHUNK.bit_length() - 1
    lvl_slabs = _slabs(_pair_levels(CHUNK))
    ys, new_states = [], []
    sweeps = [_hgrn_levels(q[:, hd * HG_DK:(hd + 1) * HG_DK], f[:, hd * HG_DK:(hd + 1) * HG_DK],
                           lvl_slabs, nlev) for hd in range(HG_HEADS)]
    swept = [None] * HG_HEADS
    for _ in range(nlev + 1):
        for hd in range(HG_HEADS):
            try:
                next(sweeps[hd])
            except StopIteration as stop:
                swept[hd] = stop.value
        yield
    for hd in range(HG_HEADS):
        sl = slice(hd * HG_DK, (hd + 1) * HG_DK)
        p, qs, ks, incl, rexc, tot = swept[hd]
        ns = len(qs)
        qd = jnp.concatenate([qs[r] * incl[r] for r in range(ns)], axis=0).astype(BF16)
        kd = jnp.concatenate([ks[r] * rexc[r] for r in range(ns)], axis=0).astype(BF16)
        st = states[hd]
        o = (jnp.dot(p.astype(BF16), v[:, sl], preferred_element_type=F32)
             + _nt_dot(qd, st.astype(BF16)))
        new_states.append(st * tot[0][0:1, :] + _tn_dot(v[:, sl], kd))
        ys.append(_hgrn_finish(o, nw[:, sl], og[:, sl]))
        yield
    return ys, new_states


def _hgrn_sample_kernel(hq_ref, f_ref, hv_ref, og_ref, nw_ref, s_ref, ya_ref, so_ref, *, seq_len):
    nlev = SEQ_PAD.bit_length() - 1
    t = lax.broadcasted_iota(jnp.int32, (CHUNK, HG_W), 0)
    f = jnp.where((t & (SEQ_PAD - 1)) < seq_len, f_ref[...], 1.0)
    q = hq_ref[...].astype(F32)
    v = hv_ref[...]
    vf = v.astype(F32)
    og = og_ref[...].astype(F32)
    nw = nw_ref[...]
    lvl_slabs = _slabs(_pair_levels(CHUNK))
    assert SEQ_PAD == SUBLANES
    for hd in range(HG_HEADS):
        sl = slice(hd * HG_DK, (hd + 1) * HG_DK)
        p, qs, ks, incl, rexc, tot = _drain(_hgrn_levels(q[:, sl], f[:, sl], lvl_slabs, nlev))
        o_inter = []
        for s in range(CHUNK // SEQ_PAD):
            rows = slice(s * SEQ_PAD, (s + 1) * SEQ_PAD)
            st = s_ref[s, hd]
            o_inter.append(jnp.dot(qs[s] * incl[s], st, preferred_element_type=F32))
            decay = jnp.broadcast_to(tot[s][0:1, :], (HG_DV, HG_DK)).T
            so_ref[s, hd] = st * decay + _tn_dot(ks[s] * rexc[s], vf[rows, sl])
        o = (jnp.dot(p.astype(BF16), v[:, sl], preferred_element_type=F32)
             + jnp.concatenate(o_inter, axis=0))
        ya_ref[:, sl] = _hgrn_finish(o, nw[:, sl], og[:, sl]).astype(BF16)


def _hgrn_sample(hq, f, hv, og, nw, state, seq_len):
    n = hq.shape[0]
    spc = CHUNK // SEQ_PAD
    blk = pl.BlockSpec((CHUNK, HG_W), lambda i: (i, 0))
    sblk = pl.BlockSpec((spc, HG_HEADS, HG_DK, HG_DV), lambda i: (i, 0, 0, 0))
    return pl.pallas_call(
        functools.partial(_hgrn_sample_kernel, seq_len=seq_len),
        grid=(n // CHUNK,),
        in_specs=[blk, blk, blk, blk, _const_spec((1, HG_W)), sblk],
        out_specs=[blk, sblk],
        out_shape=(jax.ShapeDtypeStruct((n, HG_W), BF16),
                   jax.ShapeDtypeStruct(state.shape, F32)),
        compiler_params=pltpu.CompilerParams(
            dimension_semantics=("arbitrary",), vmem_limit_bytes=VMEM_LIMIT),
        name="hgrn_sample",
    )(hq, f, hv, og, nw, state)


def _alibi_slope(head):
    return 2.0 ** (-8.0 * (head + 1) / N_Q_HEADS)


def _tile_kv_head(x, g):
    lane = lax.broadcasted_iota(jnp.int32, x.shape, 1)
    other = pltpu.roll(x, HEAD_DIM, axis=1)
    mine = (lane >> HEAD_SHIFT) == g
    one = jnp.where(mine, x, other).astype(BF16)
    return jnp.concatenate([one] * (GROUP_W // KV_W), axis=1)


def _attn_group(qg, kt, vt, bias_of, sink_of):
    nk = kt.shape[0]
    lane_head = lax.broadcasted_iota(jnp.int32, kt.shape, 1) >> HEAD_SHIFT
    zero = jnp.zeros_like(kt)
    kbd = jnp.concatenate([jnp.where(lane_head == hh, kt, zero) for hh in range(Q_PER_KV)], axis=0)
    vbd = jnp.concatenate([jnp.where(lane_head == hh, vt, zero) for hh in range(Q_PER_KV)], axis=0)
    s_all = _nt_dot(qg, kbd)
    yield
    ps, invs = [], []
    for hh in range(Q_PER_KV):
        s = s_all[:, hh * nk:(hh + 1) * nk] + bias_of(hh)
        sink = sink_of(hh)
        m = jnp.maximum(jnp.max(s, axis=-1, keepdims=True), sink)
        yield
        p = jnp.exp(s - m)
        denom = jnp.sum(p, axis=-1, keepdims=True) + jnp.exp(sink - m)
        ps.append(p.astype(BF16))
        invs.append(1.0 / denom)
        yield
    o = jnp.dot(jnp.concatenate(ps, axis=1), vbd, preferred_element_type=F32)
    out_head = lax.broadcasted_iota(jnp.int32, o.shape, 1) >> HEAD_SHIFT
    scale = invs[Q_PER_KV - 1]
    for hh in range(Q_PER_KV - 2, -1, -1):
        scale = jnp.where(out_head == hh, invs[hh], scale)
    yield
    return o * scale


def _init_band_bias(bias_ref):
    tq = lax.broadcasted_iota(jnp.int32, (WINDOW, 2 * WINDOW), 0)
    c = lax.broadcasted_iota(jnp.int32, (WINDOW, 2 * WINDOW), 1)
    dist = WINDOW + tq - c
    valid = (dist >= 0) & (dist < WINDOW)
    for h in range(N_Q_HEADS):
        bias = jnp.where(valid, -_alibi_slope(h) * dist.astype(F32), NEG_BIG)
        bias_ref[0, h] = bias
        bias_ref[1, h] = jnp.where(c < WINDOW, NEG_BIG, bias)


def _swa_block(q, k2, v2, bias_ref, table, sink_ref):
    outs = []
    for g in range(N_KV_HEADS):
        kt = _tile_kv_head(k2, g)
        vt = _tile_kv_head(v2, g)
        o = yield from _attn_group(q[:, g * GROUP_W:(g + 1) * GROUP_W], kt, vt,
                                   lambda hh, g=g: bias_ref[table, g * Q_PER_KV + hh],
                                   lambda hh, g=g: sink_ref[g * Q_PER_KV + hh])
        outs.append(o)
    return outs


SWA_SAMPLE_SEQS = 16
SWA_SAMPLE_BATCH = 4
STACK = Q_PER_KV * SEQ_PAD


def _swa_sample_kernel(sink_ref, q_ref, kn_ref, vn_ref, ckt_ref, cvt_ref,
                       y_ref, ckot_ref, cvot_ref, bias_ref, sinkv_ref, *, seq_len):
    cw = ckt_ref.shape[2]

    @pl.when(pl.program_id(0) == 0)
    def _():
        rows = SWA_SAMPLE_BATCH * STACK
        r = lax.broadcasted_iota(jnp.int32, (rows, cw), 0)
        c = lax.broadcasted_iota(jnp.int32, (rows, cw), 1)
        tq = r & (SEQ_PAD - 1)
        hh = (r >> (SEQ_PAD.bit_length() - 1)) & (Q_PER_KV - 1)
        dist_c = cw + tq - c
        valid_c = (dist_c >= 0) & (dist_c < WINDOW)
        dist_n = tq - c
        valid_n = (dist_n >= 0) & (dist_n < WINDOW) & (c < seq_len)
        for g in range(N_KV_HEADS):
            slope = jnp.full((rows, cw), _alibi_slope(g * Q_PER_KV + Q_PER_KV - 1), F32)
            sink = jnp.full((rows, cw), sink_ref[g * Q_PER_KV + Q_PER_KV - 1], F32)
            for h in range(Q_PER_KV - 2, -1, -1):
                slope = jnp.where(hh == h, _alibi_slope(g * Q_PER_KV + h), slope)
                sink = jnp.where(hh == h, sink_ref[g * Q_PER_KV + h], sink)
            bias_ref[g, 0] = jnp.where(valid_c, -slope * dist_c.astype(F32), NEG_BIG)
            bias_ref[g, 1] = jnp.where(valid_n, -slope * dist_n.astype(F32), NEG_BIG)
            sinkv_ref[g] = sink

    lane = lax.broadcasted_iota(jnp.int32, (KV_W, cw), 1)
    keep = lane < cw - seq_len
    pad = jnp.zeros((cw - SEQ_PAD, KV_W), F32)

    nb = SWA_SAMPLE_BATCH
    shift = cw - seq_len

    def body(i, carry):
        seqs = [i * nb + j for j in range(nb)]
        q = [q_ref[s].astype(F32) for s in seqs]
        k_new = [jnp.concatenate([kn_ref[s], pad], axis=0) for s in seqs]
        v_new = [jnp.concatenate([vn_ref[s], pad], axis=0) for s in seqs]
        ckt = [ckt_ref[s] for s in seqs]
        cvt = [cvt_ref[s] for s in seqs]
        for g in range(N_KV_HEADS):
            ch = slice(g * HEAD_DIM, (g + 1) * HEAD_DIM)
            qs = [jnp.concatenate(
                [qj[:, (g * Q_PER_KV + h) * HEAD_DIM:(g * Q_PER_KV + h + 1) * HEAD_DIM]
                 for h in range(Q_PER_KV)], axis=0).astype(BF16) for qj in q]
            s_c = jnp.concatenate(
                [jnp.dot(qs[j], ckt[j][ch, :].astype(BF16), preferred_element_type=F32)
                 for j in range(nb)], axis=0) + bias_ref[g, 0]
            s_n = jnp.concatenate(
                [_nt_dot(qs[j], k_new[j][:, ch].astype(BF16)) for j in range(nb)],
                axis=0) + bias_ref[g, 1]
            sink = sinkv_ref[g][:, 0:1]
            m = jnp.maximum(jnp.maximum(jnp.max(s_c, axis=-1, keepdims=True),
                                        jnp.max(s_n, axis=-1, keepdims=True)), sink)
            p_c = jnp.exp(s_c - m)
            p_n = jnp.exp(s_n - m)
            denom = (jnp.sum(p_c, axis=-1, keepdims=True) + jnp.sum(p_n, axis=-1, keepdims=True)
                     + jnp.exp(sink - m))
            inv = 1.0 / denom
            p_c, p_n = p_c.astype(BF16), p_n.astype(BF16)
            for j in range(nb):
                rows = slice(j * STACK, (j + 1) * STACK)
                o = (_nt_dot(p_c[rows, :], cvt[j][ch, :].astype(BF16))
                     + jnp.dot(p_n[rows, :], v_new[j][:, ch].astype(BF16),
                               preferred_element_type=F32)) * inv[rows, :]
                y = jnp.concatenate(
                    [o[h * SEQ_PAD:(h + 1) * SEQ_PAD, :] for h in range(Q_PER_KV)], axis=1)
                y_ref[seqs[j], :, g * GROUP_W:(g + 1) * GROUP_W] = y.astype(BF16)
        for j in range(nb):
            ckot_ref[seqs[j]] = jnp.where(keep, pltpu.roll(ckt[j], shift, axis=1),
                                          _roll_rows(k_new[j], shift).T)
            cvot_ref[seqs[j]] = jnp.where(keep, pltpu.roll(cvt[j], shift, axis=1),
                                          _roll_rows(v_new[j], shift).T)
        return carry

    lax.fori_loop(0, q_ref.shape[0] // nb, body, 0)


def _swa_sample(sinks, aq, ak, av, cache_kt, cache_vt, seq_len):
    nseq, _, cw = cache_kt.shape
    g = SWA_SAMPLE_SEQS
    blk = lambda r, w: pl.BlockSpec((g, r, w), lambda i: (i, 0, 0))
    return pl.pallas_call(
        functools.partial(_swa_sample_kernel, seq_len=seq_len),
        grid=(nseq // g,),
        in_specs=[pl.BlockSpec(memory_space=pltpu.SMEM),
                  blk(SEQ_PAD, Q_W), blk(SEQ_PAD, KV_W), blk(SEQ_PAD, KV_W),
                  blk(KV_W, cw), blk(KV_W, cw)],
        out_specs=[blk(SEQ_PAD, Q_W), blk(KV_W, cw), blk(KV_W, cw)],
        out_shape=(jax.ShapeDtypeStruct((nseq, SEQ_PAD, Q_W), BF16),
                   jax.ShapeDtypeStruct((nseq, KV_W, cw), F32),
                   jax.ShapeDtypeStruct((nseq, KV_W, cw), F32)),
        scratch_shapes=[pltpu.VMEM((N_KV_HEADS, 2, SWA_SAMPLE_BATCH * STACK, cw), F32),
                        pltpu.VMEM((N_KV_HEADS, SWA_SAMPLE_BATCH * STACK, cw), F32)],
        compiler_params=pltpu.CompilerParams(
            dimension_semantics=("arbitrary",), vmem_limit_bytes=VMEM_LIMIT),
        name="swa_sample",
    )(sinks, aq, ak, av, cache_kt, cache_vt)


def _merge_ffn_rows(x, ya, yb, ga, gb, woh_ref, woa_ref, wout_ref, n2_ref, wg_ref, wu_ref, wd_ref,
                    act_ref):
    branch_a = jnp.dot(ya, woh_ref[...], preferred_element_type=F32)
    yield
    branch_b = jnp.dot(yb, woa_ref[...], preferred_element_type=F32)
    yield
    merged = ga.astype(F32) * branch_a + gb.astype(F32) * branch_b
    x1 = x + jnp.dot(merged.astype(BF16), wout_ref[...], preferred_element_type=F32)
    yield
    h2 = (x1 * lax.rsqrt(jnp.mean(x1 * x1, axis=-1, keepdims=True) + EPS) * n2_ref[...]).astype(BF16)
    d_ff = wg_ref.shape[1]
    for c in range(d_ff // FFN_CHUNK):
        cols = slice(c * FFN_CHUNK, (c + 1) * FFN_CHUNK)
        gate = jnp.dot(h2, wg_ref[:, cols], preferred_element_type=F32)
        yield
        up = jnp.dot(h2, wu_ref[:, cols], preferred_element_type=F32)
        yield
        act_ref[:, cols] = (gate * _sigmoid(gate) * up).astype(BF16)
        yield
    act = act_ref[...]
    outs = []
    for n in range(x.shape[1] // DOWN_TILE):
        cols = slice(n * DOWN_TILE, (n + 1) * DOWN_TILE)
        outs.append(x1[:, cols] + jnp.dot(act, wd_ref[:, cols], preferred_element_type=F32))
        yield
    return jnp.concatenate(outs, axis=1)


def _merge_ffn_kernel(x_ref, ya_ref, yb_ref, ga_ref, gb_ref, woh_ref, woa_ref, wout_ref,
                      n2_ref, wg_ref, wu_ref, wd_ref, o_ref, act_s):
    o_ref[...] = _drain(_merge_ffn_rows(
        x_ref[...], ya_ref[...], yb_ref[...], ga_ref[...], gb_ref[...],
        woh_ref, woa_ref, wout_ref, n2_ref, wg_ref, wu_ref, wd_ref, act_s))


def _mixers_kernel(sink_ref, hq_ref, f_ref, hv_ref, og_ref, aq_ref, kp_ref, kc_ref, vp_ref, vc_ref,
                   nw_ref, ya_ref, yb_ref, sfin_ref, klast_ref, vlast_ref, st_ref, bias_ref, *, nb):
    s = pl.program_id(0)

    @pl.when(s == 0)
    def _():
        _init_band_bias(bias_ref)
        st_ref[...] = jnp.zeros_like(st_ref)

    pos = lax.rem(s, nb)
    first = pos == 0

    def recurrence():
        states = [jnp.where(first, 0.0, st_ref[hd]) for hd in range(HG_HEADS)]
        ys_all = []
        for sub in range(MIX_SUB):
            rows = slice(sub * CHUNK, (sub + 1) * CHUNK)
            ys, states = yield from _hgrn_chunk(
                hq_ref[rows, :].astype(F32), f_ref[rows, :], hv_ref[rows, :],
                og_ref[rows, :].astype(F32), nw_ref[...], states)
            ys_all.append(ys)
        return ys_all, states

    def attention():
        k_prev, v_prev = kp_ref[...], vp_ref[...]
        outs_all = []
        for sub in range(MIX_SUB):
            rows = slice(sub * CHUNK, (sub + 1) * CHUNK)
            k_cur, v_cur = kc_ref[rows, :], vc_ref[rows, :]
            table = jnp.where(first, 1, 0) if sub == 0 else 0
            outs = yield from _swa_block(
                aq_ref[rows, :], jnp.concatenate([k_prev, k_cur], axis=0),
                jnp.concatenate([v_prev, v_cur], axis=0), bias_ref, table, sink_ref)
            k_prev, v_prev = k_cur, v_cur
            outs_all.append(outs)
        return outs_all

    (ys_all, new_states), outs_all = _interleave(
        [recurrence(), attention()], [HGRN_PIECES, SWA_PIECES])

    for hd in range(HG_HEADS):
        st_ref[hd] = new_states[hd]
    for sub in range(MIX_SUB):
        rows = slice(sub * CHUNK, (sub + 1) * CHUNK)
        for hd in range(HG_HEADS):
            ya_ref[rows, hd * HG_DV:(hd + 1) * HG_DV] = ys_all[sub][hd].astype(BF16)
        for g in range(N_KV_HEADS):
            yb_ref[rows, g * GROUP_W:(g + 1) * GROUP_W] = outs_all[sub][g].astype(BF16)

    @pl.when(pos == nb - 1)
    def _():
        for hd in range(HG_HEADS):
            sfin_ref[0, hd] = st_ref[hd].T
        last = slice((MIX_SUB - 1) * CHUNK, MIX_SUB * CHUNK)
        klast_ref[0] = kc_ref[last, :].T
        vlast_ref[0] = vc_ref[last, :].T


def _mixers(sinks, hq, f, hv, og, aq, ak, av, nw, batch, seq):
    n = batch * seq
    rows = MIX_SUB * CHUNK
    nb = seq // rows
    cur = lambda w: pl.BlockSpec((rows, w), lambda s: (s, 0))
    prev = lambda w: pl.BlockSpec((CHUNK, w), lambda s: (jnp.maximum(MIX_SUB * s - 1, 0), 0))
    per_seq = lambda *shape: pl.BlockSpec((1,) + shape, lambda s: (s // nb,) + (0,) * len(shape))
    return pl.pallas_call(
        functools.partial(_mixers_kernel, nb=nb),
        grid=(batch * nb,),
        in_specs=[pl.BlockSpec(memory_space=pltpu.SMEM),
                  cur(HG_W), cur(HG_W), cur(HG_W), cur(HG_W),
                  cur(Q_W), prev(KV_W), cur(KV_W), prev(KV_W), cur(KV_W),
                  _const_spec((1, HG_W))],
        out_specs=[cur(HG_W), cur(Q_W), per_seq(HG_HEADS, HG_DK, HG_DV),
                   per_seq(KV_W, WINDOW), per_seq(KV_W, WINDOW)],
        out_shape=(jax.ShapeDtypeStruct((n, HG_W), BF16),
                   jax.ShapeDtypeStruct((n, Q_W), BF16),
                   jax.ShapeDtypeStruct((batch, HG_HEADS, HG_DK, HG_DV), F32),
                   jax.ShapeDtypeStruct((batch, KV_W, WINDOW), F32),
                   jax.ShapeDtypeStruct((batch, KV_W, WINDOW), F32)),
        scratch_shapes=[pltpu.VMEM((HG_HEADS, HG_DV, HG_DK), F32),
                        pltpu.VMEM((2, N_Q_HEADS, WINDOW, 2 * WINDOW), F32)],
        compiler_params=pltpu.CompilerParams(
            dimension_semantics=("arbitrary",), vmem_limit_bytes=VMEM_LIMIT),
        name="mixers",
    )(sinks, hq, f, hv, og, aq, ak, ak, av, av, nw)


def _proj_mixers_kernel(sink_ref, x_ref, n1_ref, win_ref, lbl_ref, qn_ref, kn_ref, hsel_ref, nw_ref,
                        ga_ref, gb_ref, ya_ref, yb_ref, sfin_ref, klast_ref, vlast_ref,
                        st_ref, bias_ref, hq_s, f_s, hv_s, og_s, aq_s, ak_s, av_s, kprev_s, vprev_s,
                        *, nb):
    s = pl.program_id(0)
    rows = MIX_SUB * CHUNK
    stage = (hq_s, f_s, hv_s, og_s, aq_s, ak_s, av_s)

    @pl.when(s == 0)
    def _():
        _init_band_bias(bias_ref)
        st_ref[...] = jnp.zeros_like(st_ref)
        for ref in stage + (kprev_s, vprev_s):
            ref[...] = jnp.zeros_like(ref)

    pos = lax.rem(s + nb - 1, nb)
    first = pos == 0

    hq, f, hv, og, aq, ak, av = [ref[...] for ref in stage]
    k_before, v_before = kprev_s[...], vprev_s[...]

    def recurrence():
        states = [jnp.where(first, 0.0, st_ref[hd]) for hd in range(HG_HEADS)]
        ys_all = []
        for sub in range(MIX_SUB):
            r = slice(sub * CHUNK, (sub + 1) * CHUNK)
            ys, states = yield from _hgrn_chunk(hq[r, :].astype(F32), f[r, :], hv[r, :],
                                                og[r, :].astype(F32), nw_ref[...], states)
            ys_all.append(ys)
        return ys_all, states

    def attention():
        k_prev, v_prev = k_before, v_before
        outs_all = []
        for sub in range(MIX_SUB):
            r = slice(sub * CHUNK, (sub + 1) * CHUNK)
            k_cur, v_cur = ak[r, :], av[r, :]
            table = jnp.where(first, 1, 0) if sub == 0 else 0
            outs = yield from _swa_block(
                aq[r, :], jnp.concatenate([k_prev, k_cur], axis=0),
                jnp.concatenate([v_prev, v_cur], axis=0), bias_ref, table, sink_ref)
            k_prev, v_prev = k_cur, v_cur
            outs_all.append(outs)
        return outs_all

    projection = _proj_rows(slice(0, rows), x_ref, n1_ref, win_ref, lbl_ref, qn_ref, kn_ref,
                            hsel_ref, *stage, ga_ref, gb_ref)
    _, (ys_all, new_states), outs_all = _interleave(
        [projection, recurrence(), attention()], [PROJ_PIECES, HGRN_PIECES, SWA_PIECES])

    for hd in range(HG_HEADS):
        st_ref[hd] = new_states[hd]
    for sub in range(MIX_SUB):
        r = slice(sub * CHUNK, (sub + 1) * CHUNK)
        for hd in range(HG_HEADS):
            ya_ref[r, hd * HG_DV:(hd + 1) * HG_DV] = ys_all[sub][hd].astype(BF16)
        for g in range(N_KV_HEADS):
            yb_ref[r, g * GROUP_W:(g + 1) * GROUP_W] = outs_all[sub][g].astype(BF16)
    last = slice((MIX_SUB - 1) * CHUNK, MIX_SUB * CHUNK)
    kprev_s[...] = ak[last, :]
    vprev_s[...] = av[last, :]

    @pl.when((pos == nb - 1) & (s > 0))
    def _():
        for hd in range(HG_HEADS):
            sfin_ref[0, hd] = st_ref[hd].T
        klast_ref[0] = ak[last, :].T
        vlast_ref[0] = av[last, :].T


def _proj_mixers(sinks, x, n1, win, lbl, qn, kn, hsel, nw, batch, seq):
    n, d = x.shape
    pw = win.shape[1]
    rows = MIX_SUB * CHUNK
    nb = seq // rows
    total = batch * nb
    front = lambda w: pl.BlockSpec((rows, w), lambda s: (jnp.minimum(s, total - 1), 0))
    back = lambda w: pl.BlockSpec((rows, w), lambda s: (jnp.maximum(s - 1, 0), 0))
    per_seq = lambda *shape: pl.BlockSpec(
        (1,) + shape, lambda s: (jnp.maximum(s - 1, 0) // nb,) + (0,) * len(shape))
    return pl.pallas_call(
        functools.partial(_proj_mixers_kernel, nb=nb),
        grid=(total + 1,),
        in_specs=[pl.BlockSpec(memory_space=pltpu.SMEM), front(d),
                  _const_spec((1, d)), _const_spec((d, pw)), _const_spec(lbl.shape),
                  _const_spec((1, Q_W)), _const_spec((1, KV_W)), _const_spec((Q_W, Q_W)),
                  _const_spec((1, HG_W))],
        out_specs=[front(d), front(d), back(HG_W), back(Q_W), per_seq(HG_HEADS, HG_DK, HG_DV),
                   per_seq(KV_W, WINDOW), per_seq(KV_W, WINDOW)],
        out_shape=(jax.ShapeDtypeStruct((n, d), BF16),
                   jax.ShapeDtypeStruct((n, d), BF16),
                   jax.ShapeDtypeStruct((n, HG_W), BF16),
                   jax.ShapeDtypeStruct((n, Q_W), BF16),
                   jax.ShapeDtypeStruct((batch, HG_HEADS, HG_DK, HG_DV), F32),
                   jax.ShapeDtypeStruct((batch, KV_W, WINDOW), F32),
                   jax.ShapeDtypeStruct((batch, KV_W, WINDOW), F32)),
        scratch_shapes=[pltpu.VMEM((HG_HEADS, HG_DV, HG_DK), F32),
                        pltpu.VMEM((2, N_Q_HEADS, WINDOW, 2 * WINDOW), F32),
                        pltpu.VMEM((rows, HG_W), BF16), pltpu.VMEM((rows, HG_W), F32),
                        pltpu.VMEM((rows, HG_W), BF16), pltpu.VMEM((rows, HG_W), BF16),
                        pltpu.VMEM((rows, Q_W), BF16), pltpu.VMEM((rows, KV_W), F32),
                        pltpu.VMEM((rows, KV_W), F32),
                        pltpu.VMEM((CHUNK, KV_W), F32), pltpu.VMEM((CHUNK, KV_W), F32)],
        compiler_params=pltpu.CompilerParams(
            dimension_semantics=("arbitrary",), vmem_limit_bytes=VMEM_LIMIT),
        name="proj_mixers",
    )(sinks, x, n1, win, lbl, qn, kn, hsel, nw)


def _merge_ffn(x, ya, yb, ga, gb, woh, woa, wout, n2, wg, wu, wd):
    n, d = x.shape
    d_ff = wg.shape[1]
    tm = ROW_TILE
    row = lambda w: pl.BlockSpec((tm, w), lambda i: (i, 0))
    return pl.pallas_call(
        _merge_ffn_kernel,
        grid=(n // tm,),
        in_specs=[row(d), row(HG_W), row(Q_W), row(d), row(d),
                  _const_spec((HG_W, d)), _const_spec((Q_W, d)), _const_spec((d, d)),
                  _const_spec((1, d)), _const_spec((d, d_ff)), _const_spec((d, d_ff)),
                  _const_spec((d_ff, d))],
        out_specs=row(d),
        out_shape=jax.ShapeDtypeStruct((n, d), F32),
        scratch_shapes=[pltpu.VMEM((tm, d_ff), BF16)],
        compiler_params=pltpu.CompilerParams(
            dimension_semantics=("arbitrary",), vmem_limit_bytes=VMEM_LIMIT),
        name="merge_ffn",
    )(x, ya, yb, ga, gb, woh, woa, wout, n2, wg, wu, wd)


def kernel(x_prompt, x_sample, state_hgrn, cache_k, cache_v, norm1_w, w_in, lb_logits, hgrn_norm_w,
           q_norm_w, k_norm_w, sinks, w_o_hgrn, w_o_attn, w_out, norm2_w, w_ffn_gate, w_ffn_up,
           w_ffn_down):
    depth = w_in.shape[0]
    assert depth == 1 and lb_logits.shape[0] == 2, "single-layer step only"
    batch, seq, d = x_prompt.shape
    nseq, dec_seq, _ = x_sample.shape
    cw = cache_k.shape[2]
    assert seq % ROW_TILE == 0 and seq % CHUNK == 0 and seq >= WINDOW
    assert dec_seq <= SEQ_PAD and cw == WINDOW and w_ffn_gate.shape[2] % FFN_CHUNK == 0
    assert (nseq * SEQ_PAD) % ROW_TILE == 0 and nseq % SWA_SAMPLE_SEQS == 0

    win = w_in[0].astype(BF16)
    n1 = norm1_w[0][None, :]
    n2 = norm2_w[0][None, :]
    qn = jnp.tile(q_norm_w[0], N_Q_HEADS)[None, :] * (HEAD_DIM ** -0.5)
    kn = jnp.tile(k_norm_w[0], N_KV_HEADS)[None, :]
    nw = hgrn_norm_w[0][None, :]
    head_of = jnp.arange(Q_W, dtype=jnp.int32) // HEAD_DIM
    hsel = jnp.where(head_of[:, None] == head_of[None, :], 1.0 / HEAD_DIM, 0.0).astype(BF16)
    sk = sinks[0].astype(F32)
    woh, woa, wout = w_o_hgrn[0].astype(BF16), w_o_attn[0].astype(BF16), w_out[0].astype(BF16)
    wg, wu, wd = w_ffn_gate[0].astype(BF16), w_ffn_up[0].astype(BF16), w_ffn_down[0].astype(BF16)

    xp = x_prompt.reshape(batch * seq, d)
    hq, f, hv, og, aq, ak, av, ga, gb = _proj(xp, n1, win, lb_logits, qn, kn, hsel)
    ya, yb, s_prompt, kt_prompt, vt_prompt = _mixers(sk, hq, f, hv, og, aq, ak, av, nw, batch, seq)
    y_prompt = _merge_ffn(xp, ya, yb, ga, gb, woh, woa, wout, n2, wg, wu, wd).reshape(batch, seq, d)

    def from_transposed(t):
        return t.reshape(t.shape[0], N_KV_HEADS, HEAD_DIM, t.shape[2]).transpose(0, 3, 1, 2)

    def to_transposed(c):
        return c.transpose(0, 2, 3, 1).reshape(c.shape[0], KV_W, c.shape[1])

    xs = jnp.pad(x_sample, ((0, 0), (0, SEQ_PAD - dec_seq), (0, 0))).reshape(nseq * SEQ_PAD, d)
    hq, f, hv, og, aq, ak, av, ga, gb = _proj(xs, n1, win, lb_logits, qn, kn, hsel)
    ya, s_sample = _hgrn_sample(hq, f, hv, og, nw, state_hgrn[0], dec_seq)
    yb, kt_sample, vt_sample = _swa_sample(
        sk, aq.reshape(nseq, SEQ_PAD, Q_W), ak.reshape(nseq, SEQ_PAD, KV_W),
        av.reshape(nseq, SEQ_PAD, KV_W), to_transposed(cache_k[0]), to_transposed(cache_v[0]),
        dec_seq)
    ys = _merge_ffn(xs, ya, yb.reshape(nseq * SEQ_PAD, Q_W), ga, gb, woh, woa, wout, n2, wg, wu, wd)
    y_sample = ys.reshape(nseq, SEQ_PAD, d)[:, :dec_seq]

    return (y_prompt, y_sample, s_prompt[None], from_transposed(kt_prompt)[None],
            from_transposed(vt_prompt)[None], s_sample[None], from_transposed(kt_sample)[None],
            from_transposed(vt_sample)[None])
```

```python
import functools
import math

import jax
import jax.numpy as jnp
from jax import lax
from jax.experimental import pallas as pl
from jax.experimental.pallas import tpu as pltpu

F32 = jnp.float32
BF16 = jnp.bfloat16

EPS = 1e-6
NEG_BIG = -1e30

HG_HEADS = 4
HG_DK = 128
HG_DV = 128
HG_W = HG_HEADS * HG_DK
N_Q_HEADS = 8
N_KV_HEADS = 2
Q_PER_KV = N_Q_HEADS // N_KV_HEADS
HEAD_DIM = 64
HEAD_SHIFT = HEAD_DIM.bit_length() - 1
WINDOW = 128
Q_W = N_Q_HEADS * HEAD_DIM
KV_W = N_KV_HEADS * HEAD_DIM
GROUP_W = Q_PER_KV * HEAD_DIM

SEQ_PAD = 8
ROW_TILE = 512
CHUNK = 128
FFN_CHUNK = 256
DOWN_TILE = 1024
MIX_SUB = 4
PROJ_TILE = 256
HEAD_GROUP = 2
HGRN_PIECES = MIX_SUB * 20
SWA_PIECES = MIX_SUB * 18
DIAG = 64
V7X_VMEM_BYTES = 64 * 1024 * 1024


def _nbytes(shape, dtype):
    return math.prod(shape) * jnp.dtype(dtype).itemsize


def _vmem_limit(resident, streamed, scratch, temporaries):
    need = resident + 2 * streamed + scratch + temporaries
    assert need <= V7X_VMEM_BYTES, need
    return need


def _nt_dot(a, b):
    return lax.dot_general(a, b, (((1,), (1,)), ((), ())), preferred_element_type=F32)


def _tn_dot(a, b):
    return lax.dot_general(a, b, (((0,), (0,)), ((), ())), preferred_element_type=F32)


def _sigmoid(x):
    return 1.0 / (1.0 + jnp.exp(-x))


def _drain(gen):
    try:
        while True:
            next(gen)
    except StopIteration as stop:
        return stop.value


def _interleave(gens, pieces):
    done = [0] * len(gens)
    results = [None] * len(gens)
    live = list(range(len(gens)))
    while live:
        i = min(live, key=lambda g: ((done[g] + 1) / pieces[g], g))
        try:
            next(gens[i])
            done[i] += 1
        except StopIteration as stop:
            results[i] = stop.value
            live.remove(i)
    return results


def _const_spec(shape):
    nd = len(shape)
    return pl.BlockSpec(shape, lambda *_: (0,) * nd, pipeline_mode=pl.Buffered(1))


def _proj_rows(rows, x_ref, n1_ref, win_ref, lbl_ref, qn_ref, kn_ref, hsel_ref,
               hq_ref, f_ref, hv_ref, og_ref, aq_ref, ak_ref, av_ref, ga_ref, gb_ref):
    x = x_ref[rows, :]
    h = x * lax.rsqrt(jnp.mean(x * x, axis=-1, keepdims=True) + EPS) * n1_ref[...]
    hb = h.astype(BF16)
    yield

    w = PROJ_TILE
    col = [0]

    def proj():
        res = jnp.dot(hb, win_ref[:, col[0]:col[0] + w], preferred_element_type=F32)
        col[0] += w
        return res

    lg = lbl_ref[...]
    e = jnp.exp(lg - jnp.max(lg, axis=0, keepdims=True))
    lb = e[0:1, :] / jnp.sum(e, axis=0, keepdims=True)

    for c in range(0, HG_W, w):
        qa = proj()
        yield
        hq_ref[rows, c:c + w] = (qa * _sigmoid(qa) * (HG_DK ** -0.5)).astype(BF16)
        yield
    for c in range(0, HG_W, w):
        fa = proj()
        yield
        f_ref[rows, c:c + w] = lb[:, c:c + w] + (1.0 - lb[:, c:c + w]) * _sigmoid(fa)
        yield
    for c in range(0, HG_W, w):
        hv_ref[rows, c:c + w] = proj().astype(BF16)
        yield
    for c in range(0, HG_W, w):
        go = proj()
        yield
        og_ref[rows, c:c + w] = (go * _sigmoid(go)).astype(BF16)
        yield

    for c in range(0, Q_W, w):
        qb = proj()
        yield
        msq = jnp.dot((qb * qb).astype(BF16), hsel_ref[0:w, 0:w], preferred_element_type=F32)
        aq_ref[rows, c:c + w] = (qb * lax.rsqrt(msq + EPS) * qn_ref[:, c:c + w]).astype(BF16)
        yield
    assert 2 * KV_W == w
    kv = proj()
    yield
    kb = kv[:, 0:KV_W]
    msk = jnp.dot((kb * kb).astype(BF16), hsel_ref[0:KV_W, 0:KV_W], preferred_element_type=F32)
    ak_ref[rows, :] = kb * lax.rsqrt(msk + EPS) * kn_ref[...]
    av_ref[rows, :] = kv[:, KV_W:2 * KV_W]
    yield

    d = x.shape[-1]
    for gate_ref in (ga_ref, gb_ref):
        for c in range(0, d, w):
            g = proj()
            yield
            gate_ref[rows, c:c + w] = _sigmoid(g).astype(BF16)
            yield


def _proj_kernel(*refs):
    half = ROW_TILE // 2
    first = _proj_rows(slice(0, half), *refs)
    second = _proj_rows(slice(half, ROW_TILE), *refs)
    next(first)
    _interleave([first, second], [1, 1])


def _proj(x, n1, win, lbl, qn, kn, hsel):
    n, d = x.shape
    pw = win.shape[1]
    tm = ROW_TILE
    row = lambda w: pl.BlockSpec((tm, w), lambda i: (i, 0))
    out_shapes = (
        jax.ShapeDtypeStruct((n, HG_W), BF16),
        jax.ShapeDtypeStruct((n, HG_W), F32),
        jax.ShapeDtypeStruct((n, HG_W), BF16),
        jax.ShapeDtypeStruct((n, HG_W), BF16),
        jax.ShapeDtypeStruct((n, Q_W), BF16),
        jax.ShapeDtypeStruct((n, KV_W), F32),
        jax.ShapeDtypeStruct((n, KV_W), F32),
        jax.ShapeDtypeStruct((n, d), BF16),
        jax.ShapeDtypeStruct((n, d), BF16),
    )
    vmem = _vmem_limit(
        resident=_nbytes((d, pw), BF16) + _nbytes((Q_W, Q_W), BF16),
        streamed=_nbytes((tm, d), F32) + sum(_nbytes((tm, s.shape[1]), s.dtype) for s in out_shapes),
        scratch=0,
        temporaries=3 * _nbytes((tm, d), F32))
    return pl.pallas_call(
        _proj_kernel,
        grid=(n // tm,),
        in_specs=[row(d), _const_spec((1, d)), _const_spec((d, pw)), _const_spec(lbl.shape),
                  _const_spec((1, Q_W)), _const_spec((1, KV_W)), _const_spec((Q_W, Q_W))],
        out_specs=[row(HG_W), row(HG_W), row(HG_W), row(HG_W), row(Q_W), row(KV_W), row(KV_W),
                   row(d), row(d)],
        out_shape=out_shapes,
        compiler_params=pltpu.CompilerParams(
            dimension_semantics=("arbitrary",), vmem_limit_bytes=vmem),
        name="proj",
    )(x, n1, win, lbl, qn, kn, hsel)


def _roll_rows(x, d):
    d = d % x.shape[0]
    return x if d == 0 else pltpu.roll(x, d, axis=0)


SUBLANES = 8


def _slabs(x):
    return [x[r * SUBLANES:(r + 1) * SUBLANES, :] for r in range(x.shape[0] // SUBLANES)]


def _pair_levels(rows):
    ti = lax.broadcasted_iota(jnp.int32, (rows, rows), 0)
    tj = lax.broadcasted_iota(jnp.int32, (rows, rows), 1)
    lv = 31 - lax.clz(ti ^ tj)
    return jnp.where(ti > tj, lv, jnp.where(ti == tj, DIAG, -1))


def _hgrn_levels(qh, fh, lvl_slabs, nlev):
    ns = qh.shape[0] // SUBLANES
    kh = 1.0 - fh
    qs, ks = _slabs(qh), _slabs(kh)
    incl, tot, rexc = _slabs(fh), _slabs(fh), [None] * ns
    t8 = lax.broadcasted_iota(jnp.int32, (SUBLANES, qh.shape[1]), 0)
    zero = jnp.zeros((SUBLANES, qh.shape[1]), F32)
    qb, kb = qh.astype(BF16), kh.astype(BF16)
    diag = _slabs(_nt_dot(qb, kb))
    p = [jnp.where(lvl_slabs[r] == DIAG, diag[r], 0.0) for r in range(ns)]
    for m in range(nlev):
        h = 1 << m
        hs = h // SUBLANES
        upper = [hs == 0 or (r // hs) % 2 == 1 for r in range(ns)]
        lower = [hs == 0 or (r // hs) % 2 == 0 for r in range(ns)]
        a = jnp.concatenate([qs[r] * incl[r] if upper[r] else zero for r in range(ns)], axis=0)
        if m == 0:
            b = kb
        else:
            b = jnp.concatenate([ks[r] * rexc[r] if lower[r] else zero for r in range(ns)],
                                axis=0).astype(BF16)
        sc = _slabs(_nt_dot(a.astype(BF16), b))
        p = [jnp.where(lvl_slabs[r] == m, sc[r], p[r]) if upper[r] else p[r] for r in range(ns)]
        yield
        if hs == 0:
            up = (t8 & h) != 0
            for r in range(ns):
                below = pltpu.roll(tot[r], h, axis=0)
                above = pltpu.roll(tot[r], SUBLANES - h, axis=0)
                incl[r] = incl[r] * jnp.where(up, below, 1.0)
                grow = jnp.where(up, 1.0, above)
                rexc[r] = grow if rexc[r] is None else rexc[r] * grow
                tot[r] = tot[r] * jnp.where(up, below, above)
        else:
            for b0 in range(0, ns, 2 * hs):
                t_lo, t_hi = tot[b0], tot[b0 + hs]
                both = t_lo * t_hi
                for r in range(b0, b0 + hs):
                    rexc[r] = rexc[r] * t_hi
                    tot[r] = both
                for r in range(b0 + hs, b0 + 2 * hs):
                    incl[r] = incl[r] * t_lo
                    tot[r] = both
    return jnp.concatenate(p, axis=0), qs, ks, incl, rexc, tot


def _hgrn_finish(o, nw, og):
    return o * lax.rsqrt(jnp.mean(o * o, axis=-1, keepdims=True) + EPS) * nw * og


def _hgrn_chunk(q, f, v, og, nw, states):
    nlev = CHUNK.bit_length() - 1
    lvl_slabs = _slabs(_pair_levels(CHUNK))
    ys, new_states = [], []
    for h0 in range(0, HG_HEADS, HEAD_GROUP):
        group = range(h0, h0 + HEAD_GROUP)
        sweeps = {hd: _hgrn_levels(q[:, hd * HG_DK:(hd + 1) * HG_DK],
                                   f[:, hd * HG_DK:(hd + 1) * HG_DK], lvl_slabs, nlev)
                  for hd in group}
        swept = {}
        for _ in range(nlev + 1):
            for hd in group:
                try:
                    next(sweeps[hd])
                except StopIteration as stop:
                    swept[hd] = stop.value
            yield
        for hd in group:
            sl = slice(hd * HG_DK, (hd + 1) * HG_DK)
            p, qs, ks, incl, rexc, tot = swept[hd]
            ns = len(qs)
            qd = jnp.concatenate([qs[r] * incl[r] for r in range(ns)], axis=0).astype(BF16)
            kd = jnp.concatenate([ks[r] * rexc[r] for r in range(ns)], axis=0).astype(BF16)
            st = states[hd]
            o = (jnp.dot(p.astype(BF16), v[:, sl], preferred_element_type=F32)
                 + _nt_dot(qd, st.astype(BF16)))
            new_states.append(st * tot[0][0:1, :] + _tn_dot(v[:, sl], kd))
            ys.append(_hgrn_finish(o, nw[:, sl], og[:, sl]))
            yield
    return ys, new_states


def _hgrn_sample_kernel(hq_ref, f_ref, hv_ref, og_ref, nw_ref, s_ref, ya_ref, so_ref, *, seq_len):
    nlev = SEQ_PAD.bit_length() - 1
    t = lax.broadcasted_iota(jnp.int32, (CHUNK, HG_W), 0)
    f = jnp.where((t & (SEQ_PAD - 1)) < seq_len, f_ref[...], 1.0)
    q = hq_ref[...].astype(F32)
    v = hv_ref[...]
    vf = v.astype(F32)
    og = og_ref[...].astype(F32)
    nw = nw_ref[...]
    lvl_slabs = _slabs(_pair_levels(CHUNK))
    assert SEQ_PAD == SUBLANES
    for hd in range(HG_HEADS):
        sl = slice(hd * HG_DK, (hd + 1) * HG_DK)
        p, qs, ks, incl, rexc, tot = _drain(_hgrn_levels(q[:, sl], f[:, sl], lvl_slabs, nlev))
        o_inter = []
        for s in range(CHUNK // SEQ_PAD):
            rows = slice(s * SEQ_PAD, (s + 1) * SEQ_PAD)
            st = s_ref[s, hd]
            o_inter.append(jnp.dot(qs[s] * incl[s], st, preferred_element_type=F32))
            decay = jnp.broadcast_to(tot[s][0:1, :], (HG_DV, HG_DK)).T
            so_ref[s, hd] = st * decay + _tn_dot(ks[s] * rexc[s], vf[rows, sl])
        o = (jnp.dot(p.astype(BF16), v[:, sl], preferred_element_type=F32)
             + jnp.concatenate(o_inter, axis=0))
        ya_ref[:, sl] = _hgrn_finish(o, nw[:, sl], og[:, sl]).astype(BF16)


def _hgrn_sample(hq, f, hv, og, nw, state, seq_len):
    n = hq.shape[0]
    spc = CHUNK // SEQ_PAD
    blk = pl.BlockSpec((CHUNK, HG_W), lambda i: (i, 0))
    sblk = pl.BlockSpec((spc, HG_HEADS, HG_DK, HG_DV), lambda i: (i, 0, 0, 0))
    state_block = _nbytes((spc, HG_HEADS, HG_DK, HG_DV), F32)
    vmem = _vmem_limit(
        resident=0,
        streamed=2 * state_block + _nbytes((CHUNK, HG_W), F32) + 4 * _nbytes((CHUNK, HG_W), BF16),
        scratch=0,
        temporaries=16 * _nbytes((CHUNK, HG_W), F32))
    return pl.pallas_call(
        functools.partial(_hgrn_sample_kernel, seq_len=seq_len),
        grid=(n // CHUNK,),
        in_specs=[blk, blk, blk, blk, _const_spec((1, HG_W)), sblk],
        out_specs=[blk, sblk],
        out_shape=(jax.ShapeDtypeStruct((n, HG_W), BF16),
                   jax.ShapeDtypeStruct(state.shape, F32)),
        compiler_params=pltpu.CompilerParams(
            dimension_semantics=("arbitrary",), vmem_limit_bytes=vmem),
        name="hgrn_sample",
    )(hq, f, hv, og, nw, state)


LOG2E = 1.4426950408889634


def _alibi_slope(head):
    return LOG2E * 2.0 ** (-8.0 * (head + 1) / N_Q_HEADS)


def _tile_kv_head(x, g):
    lane = lax.broadcasted_iota(jnp.int32, x.shape, 1)
    other = pltpu.roll(x, HEAD_DIM, axis=1)
    mine = (lane >> HEAD_SHIFT) == g
    one = jnp.where(mine, x, other).astype(BF16)
    return jnp.concatenate([one] * (GROUP_W // KV_W), axis=1)


def _attn_group(qg, kt, vt, bias_of, sink_of):
    lane_head = lax.broadcasted_iota(jnp.int32, kt.shape, 1) >> HEAD_SHIFT
    zero = jnp.zeros_like(kt)
    vbd = jnp.concatenate([jnp.where(lane_head == hh, vt, zero) for hh in range(Q_PER_KV)], axis=0)
    ps, invs = [], []
    for hh in range(Q_PER_KV):
        s = _nt_dot(qg, jnp.where(lane_head == hh, kt, zero)) + bias_of(hh)
        sink = sink_of(hh) * LOG2E
        m = jnp.maximum(jnp.max(s, axis=-1, keepdims=True), sink)
        yield
        p = jnp.exp2(s - m)
        denom = jnp.sum(p, axis=-1, keepdims=True) + jnp.exp2(sink - m)
        ps.append(p.astype(BF16))
        invs.append(1.0 / denom)
        yield
    o = jnp.dot(jnp.concatenate(ps, axis=1), vbd, preferred_element_type=F32)
    out_head = lax.broadcasted_iota(jnp.int32, o.shape, 1) >> HEAD_SHIFT
    scale = invs[Q_PER_KV - 1]
    for hh in range(Q_PER_KV - 2, -1, -1):
        scale = jnp.where(out_head == hh, invs[hh], scale)
    yield
    return o * scale


def _init_band_bias(bias_ref):
    tq = lax.broadcasted_iota(jnp.int32, (WINDOW, 2 * WINDOW), 0)
    c = lax.broadcasted_iota(jnp.int32, (WINDOW, 2 * WINDOW), 1)
    dist = WINDOW + tq - c
    valid = (dist >= 0) & (dist < WINDOW)
    for h in range(N_Q_HEADS):
        bias = jnp.where(valid, -_alibi_slope(h) * dist.astype(F32), NEG_BIG)
        bias_ref[0, h] = bias
        bias_ref[1, h] = jnp.where(c < WINDOW, NEG_BIG, bias)


def _swa_block(q, k2, v2, bias_ref, table, sink_ref):
    outs = []
    for g in range(N_KV_HEADS):
        kt = _tile_kv_head(k2, g)
        vt = _tile_kv_head(v2, g)
        o = yield from _attn_group(q[:, g * GROUP_W:(g + 1) * GROUP_W], kt, vt,
                                   lambda hh, g=g: bias_ref[table, g * Q_PER_KV + hh],
                                   lambda hh, g=g: sink_ref[g * Q_PER_KV + hh])
        outs.append(o)
    return outs


SWA_SAMPLE_SEQS = 16
SWA_SAMPLE_BATCH = 8
STACK = Q_PER_KV * SEQ_PAD


def _swa_sample_kernel(sink_ref, q_ref, kn_ref, vn_ref, ckt_ref, cvt_ref,
                       y_ref, ckot_ref, cvot_ref, bias_ref, sinkv_ref, *, seq_len):
    cw = ckt_ref.shape[2]

    @pl.when(pl.program_id(0) == 0)
    def _():
        rows = SWA_SAMPLE_BATCH * STACK
        r = lax.broadcasted_iota(jnp.int32, (rows, cw), 0)
        c = lax.broadcasted_iota(jnp.int32, (rows, cw), 1)
        tq = r & (SEQ_PAD - 1)
        hh = (r >> (SEQ_PAD.bit_length() - 1)) & (Q_PER_KV - 1)
        dist_c = cw + tq - c
        valid_c = (dist_c >= 0) & (dist_c < WINDOW)
        dist_n = tq - c
        valid_n = (dist_n >= 0) & (dist_n < WINDOW) & (c < seq_len)
        for g in range(N_KV_HEADS):
            slope = jnp.full((rows, cw), _alibi_slope(g * Q_PER_KV + Q_PER_KV - 1), F32)
            sink = jnp.full((rows, cw), sink_ref[g * Q_PER_KV + Q_PER_KV - 1] * LOG2E, F32)
            for h in range(Q_PER_KV - 2, -1, -1):
                slope = jnp.where(hh == h, _alibi_slope(g * Q_PER_KV + h), slope)
                sink = jnp.where(hh == h, sink_ref[g * Q_PER_KV + h] * LOG2E, sink)
            bias_ref[g, 0] = jnp.where(valid_c, -slope * dist_c.astype(F32), NEG_BIG)
            bias_ref[g, 1] = jnp.where(valid_n, -slope * dist_n.astype(F32), NEG_BIG)
            sinkv_ref[g] = sink

    lane = lax.broadcasted_iota(jnp.int32, (KV_W, cw), 1)
    keep = lane < cw - seq_len
    pad = jnp.zeros((cw - SEQ_PAD, KV_W), F32)

    nb = SWA_SAMPLE_BATCH
    shift = cw - seq_len

    def body(i, carry):
        seqs = [i * nb + j for j in range(nb)]
        q = [q_ref[s].astype(F32) for s in seqs]
        k_new = [jnp.concatenate([kn_ref[s], pad], axis=0) for s in seqs]
        v_new = [jnp.concatenate([vn_ref[s], pad], axis=0) for s in seqs]
        ckt = [ckt_ref[s] for s in seqs]
        cvt = [cvt_ref[s] for s in seqs]
        for g in range(N_KV_HEADS):
            ch = slice(g * HEAD_DIM, (g + 1) * HEAD_DIM)
            qs = [jnp.concatenate(
                [qj[:, (g * Q_PER_KV + h) * HEAD_DIM:(g * Q_PER_KV + h + 1) * HEAD_DIM]
                 for h in range(Q_PER_KV)], axis=0).astype(BF16) for qj in q]
            s_c = jnp.concatenate(
                [jnp.dot(qs[j], ckt[j][ch, :].astype(BF16), preferred_element_type=F32)
                 for j in range(nb)], axis=0) + bias_ref[g, 0]
            s_n = jnp.concatenate(
                [_nt_dot(qs[j], k_new[j][:, ch].astype(BF16)) for j in range(nb)],
                axis=0) + bias_ref[g, 1]
            sink = sinkv_ref[g][:, 0:1]
            m = jnp.maximum(jnp.maximum(jnp.max(s_c, axis=-1, keepdims=True),
                                        jnp.max(s_n, axis=-1, keepdims=True)), sink)
            p_c = jnp.exp2(s_c - m)
            p_n = jnp.exp2(s_n - m)
            denom = (jnp.sum(p_c, axis=-1, keepdims=True) + jnp.sum(p_n, axis=-1, keepdims=True)
                     + jnp.exp2(sink - m))
            inv = 1.0 / denom
            p_c, p_n = p_c.astype(BF16), p_n.astype(BF16)
            for j in range(nb):
                rows = slice(j * STACK, (j + 1) * STACK)
                o = (_nt_dot(p_c[rows, :], cvt[j][ch, :].astype(BF16))
                     + jnp.dot(p_n[rows, :], v_new[j][:, ch].astype(BF16),
                               preferred_element_type=F32)) * inv[rows, :]
                y = jnp.concatenate(
                    [o[h * SEQ_PAD:(h + 1) * SEQ_PAD, :] for h in range(Q_PER_KV)], axis=1)
                y_ref[seqs[j], :, g * GROUP_W:(g + 1) * GROUP_W] = y.astype(BF16)
        for j in range(nb):
            ckot_ref[seqs[j]] = jnp.where(keep, pltpu.roll(ckt[j], shift, axis=1),
                                          _roll_rows(k_new[j], shift).T)
            cvot_ref[seqs[j]] = jnp.where(keep, pltpu.roll(cvt[j], shift, axis=1),
                                          _roll_rows(v_new[j], shift).T)
        return carry

    lax.fori_loop(0, q_ref.shape[0] // nb, body, 0)


def _swa_sample(sinks, aq, ak, av, cache_kt, cache_vt, seq_len):
    nseq, _, cw = cache_kt.shape
    g = SWA_SAMPLE_SEQS
    blk = lambda r, w: pl.BlockSpec((g, r, w), lambda i: (i, 0, 0))
    rows = SWA_SAMPLE_BATCH * STACK
    table = _nbytes((N_KV_HEADS, rows, cw), F32)
    vmem = _vmem_limit(
        resident=0,
        streamed=(4 * _nbytes((g, KV_W, cw), F32) + 2 * _nbytes((g, SEQ_PAD, Q_W), BF16)
                  + 2 * _nbytes((g, SEQ_PAD, KV_W), F32)),
        scratch=3 * table,
        temporaries=(4 * SWA_SAMPLE_BATCH + 8) * _nbytes((KV_W, cw), F32))
    return pl.pallas_call(
        functools.partial(_swa_sample_kernel, seq_len=seq_len),
        grid=(nseq // g,),
        in_specs=[pl.BlockSpec(memory_space=pltpu.SMEM),
                  blk(SEQ_PAD, Q_W), blk(SEQ_PAD, KV_W), blk(SEQ_PAD, KV_W),
                  blk(KV_W, cw), blk(KV_W, cw)],
        out_specs=[blk(SEQ_PAD, Q_W), blk(KV_W, cw), blk(KV_W, cw)],
        out_shape=(jax.ShapeDtypeStruct((nseq, SEQ_PAD, Q_W), BF16),
                   jax.ShapeDtypeStruct((nseq, KV_W, cw), F32),
                   jax.ShapeDtypeStruct((nseq, KV_W, cw), F32)),
        scratch_shapes=[pltpu.VMEM((N_KV_HEADS, 2, rows, cw), F32),
                        pltpu.VMEM((N_KV_HEADS, rows, cw), F32)],
        compiler_params=pltpu.CompilerParams(
            dimension_semantics=("arbitrary",), vmem_limit_bytes=vmem),
        name="swa_sample",
    )(sinks, aq, ak, av, cache_kt, cache_vt)


def _merge_ffn_rows(x, ya, yb, ga, gb, woh_ref, woa_ref, wout_ref, n2_ref, wg_ref, wu_ref, wd_ref,
                    act_ref):
    branch_a = jnp.dot(ya, woh_ref[...], preferred_element_type=F32)
    yield
    branch_b = jnp.dot(yb, woa_ref[...], preferred_element_type=F32)
    yield
    merged = ga.astype(F32) * branch_a + gb.astype(F32) * branch_b
    x1 = x + jnp.dot(merged.astype(BF16), wout_ref[...], preferred_element_type=F32)
    yield
    h2 = (x1 * lax.rsqrt(jnp.mean(x1 * x1, axis=-1, keepdims=True) + EPS) * n2_ref[...]).astype(BF16)
    d_ff = wg_ref.shape[1]
    for c in range(d_ff // FFN_CHUNK):
        cols = slice(c * FFN_CHUNK, (c + 1) * FFN_CHUNK)
        gate = jnp.dot(h2, wg_ref[:, cols], preferred_element_type=F32)
        yield
        up = jnp.dot(h2, wu_ref[:, cols], preferred_element_type=F32)
        yield
        act_ref[:, cols] = (gate * _sigmoid(gate) * up).astype(BF16)
        yield
    act = act_ref[...]
    outs = []
    for n in range(x.shape[1] // DOWN_TILE):
        cols = slice(n * DOWN_TILE, (n + 1) * DOWN_TILE)
        outs.append(x1[:, cols] + jnp.dot(act, wd_ref[:, cols], preferred_element_type=F32))
        yield
    return jnp.concatenate(outs, axis=1)


def _merge_ffn_kernel(x_ref, ya_ref, yb_ref, ga_ref, gb_ref, woh_ref, woa_ref, wout_ref,
                      n2_ref, wg_ref, wu_ref, wd_ref, o_ref, act_s):
    o_ref[...] = _drain(_merge_ffn_rows(
        x_ref[...], ya_ref[...], yb_ref[...], ga_ref[...], gb_ref[...],
        woh_ref, woa_ref, wout_ref, n2_ref, wg_ref, wu_ref, wd_ref, act_s))


def _mixers_kernel(sink_ref, hq_ref, f_ref, hv_ref, og_ref, aq_ref, kp_ref, kc_ref, vp_ref, vc_ref,
                   nw_ref, ya_ref, yb_ref, sfin_ref, klast_ref, vlast_ref, st_ref, bias_ref, *, nb):
    s = pl.program_id(0)

    @pl.when(s == 0)
    def _():
        _init_band_bias(bias_ref)
        st_ref[...] = jnp.zeros_like(st_ref)

    pos = lax.rem(s, nb)
    first = pos == 0

    def recurrence():
        states = [jnp.where(first, 0.0, st_ref[hd]) for hd in range(HG_HEADS)]
        ys_all = []
        for sub in range(MIX_SUB):
            rows = slice(sub * CHUNK, (sub + 1) * CHUNK)
            ys, states = yield from _hgrn_chunk(
                hq_ref[rows, :].astype(F32), f_ref[rows, :], hv_ref[rows, :],
                og_ref[rows, :].astype(F32), nw_ref[...], states)
            ys_all.append(ys)
        return ys_all, states

    def attention():
        k_prev, v_prev = kp_ref[...], vp_ref[...]
        outs_all = []
        for sub in range(MIX_SUB):
            rows = slice(sub * CHUNK, (sub + 1) * CHUNK)
            k_cur, v_cur = kc_ref[rows, :], vc_ref[rows, :]
            table = jnp.where(first, 1, 0) if sub == 0 else 0
            outs = yield from _swa_block(
                aq_ref[rows, :], jnp.concatenate([k_prev, k_cur], axis=0),
                jnp.concatenate([v_prev, v_cur], axis=0), bias_ref, table, sink_ref)
            k_prev, v_prev = k_cur, v_cur
            outs_all.append(outs)
        return outs_all

    (ys_all, new_states), outs_all = _interleave(
        [recurrence(), attention()], [HGRN_PIECES, SWA_PIECES])

    for hd in range(HG_HEADS):
        st_ref[hd] = new_states[hd]
    for sub in range(MIX_SUB):
        rows = slice(sub * CHUNK, (sub + 1) * CHUNK)
        for hd in range(HG_HEADS):
            ya_ref[rows, hd * HG_DV:(hd + 1) * HG_DV] = ys_all[sub][hd].astype(BF16)
        for g in range(N_KV_HEADS):
            yb_ref[rows, g * GROUP_W:(g + 1) * GROUP_W] = outs_all[sub][g].astype(BF16)

    @pl.when(pos == nb - 1)
    def _():
        for hd in range(HG_HEADS):
            sfin_ref[0, hd] = st_ref[hd].T
        last = slice((MIX_SUB - 1) * CHUNK, MIX_SUB * CHUNK)
        klast_ref[0] = kc_ref[last, :].T
        vlast_ref[0] = vc_ref[last, :].T


def _mixers(sinks, hq, f, hv, og, aq, ak, av, nw, batch, seq):
    n = batch * seq
    rows = MIX_SUB * CHUNK
    nb = seq // rows
    cur = lambda w: pl.BlockSpec((rows, w), lambda s: (s, 0))
    prev = lambda w: pl.BlockSpec((CHUNK, w), lambda s: (jnp.maximum(MIX_SUB * s - 1, 0), 0))
    per_seq = lambda *shape: pl.BlockSpec((1,) + shape, lambda s: (s // nb,) + (0,) * len(shape))
    state = _nbytes((HG_HEADS, HG_DV, HG_DK), F32)
    bias = _nbytes((2, N_Q_HEADS, WINDOW, 2 * WINDOW), F32)
    vmem = _vmem_limit(
        resident=0,
        streamed=(_nbytes((rows, HG_W), F32) + 6 * _nbytes((rows, HG_W), BF16)
                  + 2 * _nbytes((rows + CHUNK, KV_W), F32) + state + 2 * _nbytes((KV_W, WINDOW), F32)),
        scratch=state + bias,
        temporaries=MIX_SUB * 10 * _nbytes((CHUNK, HG_W), F32))
    return pl.pallas_call(
        functools.partial(_mixers_kernel, nb=nb),
        grid=(batch * nb,),
        in_specs=[pl.BlockSpec(memory_space=pltpu.SMEM),
                  cur(HG_W), cur(HG_W), cur(HG_W), cur(HG_W),
                  cur(Q_W), prev(KV_W), cur(KV_W), prev(KV_W), cur(KV_W),
                  _const_spec((1, HG_W))],
        out_specs=[cur(HG_W), cur(Q_W), per_seq(HG_HEADS, HG_DK, HG_DV),
                   per_seq(KV_W, WINDOW), per_seq(KV_W, WINDOW)],
        out_shape=(jax.ShapeDtypeStruct((n, HG_W), BF16),
                   jax.ShapeDtypeStruct((n, Q_W), BF16),
                   jax.ShapeDtypeStruct((batch, HG_HEADS, HG_DK, HG_DV), F32),
                   jax.ShapeDtypeStruct((batch, KV_W, WINDOW), F32),
                   jax.ShapeDtypeStruct((batch, KV_W, WINDOW), F32)),
        scratch_shapes=[pltpu.VMEM((HG_HEADS, HG_DV, HG_DK), F32),
                        pltpu.VMEM((2, N_Q_HEADS, WINDOW, 2 * WINDOW), F32)],
        compiler_params=pltpu.CompilerParams(
            dimension_semantics=("arbitrary",), vmem_limit_bytes=vmem),
        name="mixers",
    )(sinks, hq, f, hv, og, aq, ak, ak, av, av, nw)


def _merge_ffn(x, ya, yb, ga, gb, woh, woa, wout, n2, wg, wu, wd):
    n, d = x.shape
    d_ff = wg.shape[1]
    tm = ROW_TILE
    row = lambda w: pl.BlockSpec((tm, w), lambda i: (i, 0))
    vmem = _vmem_limit(
        resident=_nbytes((HG_W + Q_W + d, d), BF16) + 3 * _nbytes((d, d_ff), BF16),
        streamed=2 * _nbytes((tm, d), F32) + _nbytes((tm, HG_W + Q_W + 2 * d), BF16),
        scratch=_nbytes((tm, d_ff), BF16),
        temporaries=5 * _nbytes((tm, d), F32))
    return pl.pallas_call(
        _merge_ffn_kernel,
        grid=(n // tm,),
        in_specs=[row(d), row(HG_W), row(Q_W), row(d), row(d),
                  _const_spec((HG_W, d)), _const_spec((Q_W, d)), _const_spec((d, d)),
                  _const_spec((1, d)), _const_spec((d, d_ff)), _const_spec((d, d_ff)),
                  _const_spec((d_ff, d))],
        out_specs=row(d),
        out_shape=jax.ShapeDtypeStruct((n, d), F32),
        scratch_shapes=[pltpu.VMEM((tm, d_ff), BF16)],
        compiler_params=pltpu.CompilerParams(
            dimension_semantics=("arbitrary",), vmem_limit_bytes=vmem),
        name="merge_ffn",
    )(x, ya, yb, ga, gb, woh, woa, wout, n2, wg, wu, wd)


def kernel(x_prompt, x_sample, state_hgrn, cache_k, cache_v, norm1_w, w_in, lb_logits, hgrn_norm_w,
           q_norm_w, k_norm_w, sinks, w_o_hgrn, w_o_attn, w_out, norm2_w, w_ffn_gate, w_ffn_up,
           w_ffn_down):
    depth = w_in.shape[0]
    assert depth == 1 and lb_logits.shape[0] == 2, "single-layer step only"
    batch, seq, d = x_prompt.shape
    nseq, dec_seq, _ = x_sample.shape
    cw = cache_k.shape[2]
    assert seq % ROW_TILE == 0 and seq % CHUNK == 0 and seq >= WINDOW
    assert dec_seq <= SEQ_PAD and cw == WINDOW and w_ffn_gate.shape[2] % FFN_CHUNK == 0
    assert (nseq * SEQ_PAD) % ROW_TILE == 0 and nseq % SWA_SAMPLE_SEQS == 0

    win = w_in[0].astype(BF16)
    n1 = norm1_w[0][None, :]
    n2 = norm2_w[0][None, :]
    qn = jnp.tile(q_norm_w[0], N_Q_HEADS)[None, :] * (HEAD_DIM ** -0.5 * LOG2E)
    kn = jnp.tile(k_norm_w[0], N_KV_HEADS)[None, :]
    nw = hgrn_norm_w[0][None, :]
    head_of = jnp.arange(Q_W, dtype=jnp.int32) // HEAD_DIM
    hsel = jnp.where(head_of[:, None] == head_of[None, :], 1.0 / HEAD_DIM, 0.0).astype(BF16)
    sk = sinks[0].astype(F32)
    woh, woa, wout = w_o_hgrn[0].astype(BF16), w_o_attn[0].astype(BF16), w_out[0].astype(BF16)
    wg, wu, wd = w_ffn_gate[0].astype(BF16), w_ffn_up[0].astype(BF16), w_ffn_down[0].astype(BF16)

    xp = x_prompt.reshape(batch * seq, d)
    hq, f, hv, og, aq, ak, av, ga, gb = _proj(xp, n1, win, lb_logits, qn, kn, hsel)
    ya, yb, s_prompt, kt_prompt, vt_prompt = _mixers(sk, hq, f, hv, og, aq, ak, av, nw, batch, seq)
    y_prompt = _merge_ffn(xp, ya, yb, ga, gb, woh, woa, wout, n2, wg, wu, wd).reshape(batch, seq, d)

    def from_transposed(t):
        return t.reshape(t.shape[0], N_KV_HEADS, HEAD_DIM, t.shape[2]).transpose(0, 3, 1, 2)

    def to_transposed(c):
        return c.transpose(0, 2, 3, 1).reshape(c.shape[0], KV_W, c.shape[1])

    xs = jnp.pad(x_sample, ((0, 0), (0, SEQ_PAD - dec_seq), (0, 0))).reshape(nseq * SEQ_PAD, d)
    hq, f, hv, og, aq, ak, av, ga, gb = _proj(xs, n1, win, lb_logits, qn, kn, hsel)
    ya, s_sample = _hgrn_sample(hq, f, hv, og, nw, state_hgrn[0], dec_seq)
    yb, kt_sample, vt_sample = _swa_sample(
        sk, aq.reshape(nseq, SEQ_PAD, Q_W), ak.reshape(nseq, SEQ_PAD, KV_W),
        av.reshape(nseq, SEQ_PAD, KV_W), to_transposed(cache_k[0]), to_transposed(cache_v[0]),
        dec_seq)
    ys = _merge_ffn(xs, ya, yb.reshape(nseq * SEQ_PAD, Q_W), ga, gb, woh, woa, wout, n2, wg, wu, wd)
    y_sample = ys.reshape(nseq, SEQ_PAD, d)[:, :dec_seq]

    return (y_prompt, y_sample, s_prompt[None], from_transposed(kt_prompt)[None],
            from_transposed(vt_prompt)[None], s_sample[None], from_transposed(kt_sample)[None],
            from_transposed(vt_sample)[None])
```

```python
import functools
import math

import jax
import jax.numpy as jnp
from jax import lax
from jax.experimental import pallas as pl
from jax.experimental.pallas import tpu as pltpu

F32 = jnp.float32
BF16 = jnp.bfloat16

EPS = 1e-6
NEG_BIG = -1e30

HG_HEADS = 4
HG_DK = 128
HG_DV = 128
HG_W = HG_HEADS * HG_DK
N_Q_HEADS = 8
N_KV_HEADS = 2
Q_PER_KV = N_Q_HEADS // N_KV_HEADS
HEAD_DIM = 64
HEAD_SHIFT = HEAD_DIM.bit_length() - 1
WINDOW = 128
Q_W = N_Q_HEADS * HEAD_DIM
KV_W = N_KV_HEADS * HEAD_DIM
GROUP_W = Q_PER_KV * HEAD_DIM

SEQ_PAD = 8
ROW_TILE = 512
CHUNK = 128
FFN_CHUNK = 256
DOWN_TILE = 1024
MIX_SUB = 4
PROJ_TILE = 256
HEAD_GROUP = 2
HGRN_PIECES = MIX_SUB * 20
SWA_PIECES = MIX_SUB * 18
DIAG = 64
V7X_VMEM_BYTES = 64 * 1024 * 1024


def _nbytes(shape, dtype):
    return math.prod(shape) * jnp.dtype(dtype).itemsize


def _vmem_limit(resident, streamed, scratch, temporaries):
    need = resident + 2 * streamed + scratch + temporaries
    assert need <= V7X_VMEM_BYTES, need
    return need


def _nt_dot(a, b):
    return lax.dot_general(a, b, (((1,), (1,)), ((), ())), preferred_element_type=F32)


def _tn_dot(a, b):
    return lax.dot_general(a, b, (((0,), (0,)), ((), ())), preferred_element_type=F32)


def _sigmoid(x):
    return 1.0 / (1.0 + jnp.exp(-x))


def _drain(gen):
    try:
        while True:
            next(gen)
    except StopIteration as stop:
        return stop.value


def _interleave(gens, pieces):
    done = [0] * len(gens)
    results = [None] * len(gens)
    live = list(range(len(gens)))
    while live:
        i = min(live, key=lambda g: ((done[g] + 1) / pieces[g], g))
        try:
            next(gens[i])
            done[i] += 1
        except StopIteration as stop:
            results[i] = stop.value
            live.remove(i)
    return results


def _const_spec(shape):
    nd = len(shape)
    return pl.BlockSpec(shape, lambda *_: (0,) * nd, pipeline_mode=pl.Buffered(1))


def _proj_rows(rows, x_ref, n1_ref, win_ref, lbl_ref, qn_ref, kn_ref, hsel_ref,
               hq_ref, f_ref, hv_ref, og_ref, aq_ref, ak_ref, av_ref, ga_ref, gb_ref):
    x = x_ref[rows, :]
    h = x * lax.rsqrt(jnp.mean(x * x, axis=-1, keepdims=True) + EPS) * n1_ref[...]
    hb = h.astype(BF16)
    yield

    w = PROJ_TILE
    col = [0]

    def proj():
        res = jnp.dot(hb, win_ref[:, col[0]:col[0] + w], preferred_element_type=F32)
        col[0] += w
        return res

    lg = lbl_ref[...]
    e = jnp.exp(lg - jnp.max(lg, axis=0, keepdims=True))
    lb = e[0:1, :] / jnp.sum(e, axis=0, keepdims=True)

    for c in range(0, HG_W, w):
        qa = proj()
        yield
        hq_ref[rows, c:c + w] = (qa * _sigmoid(qa) * (HG_DK ** -0.5)).astype(BF16)
        yield
    for c in range(0, HG_W, w):
        fa = proj()
        yield
        f_ref[rows, c:c + w] = lb[:, c:c + w] + (1.0 - lb[:, c:c + w]) * _sigmoid(fa)
        yield
    for c in range(0, HG_W, w):
        hv_ref[rows, c:c + w] = proj().astype(BF16)
        yield
    for c in range(0, HG_W, w):
        go = proj()
        yield
        og_ref[rows, c:c + w] = (go * _sigmoid(go)).astype(BF16)
        yield

    for c in range(0, Q_W, w):
        qb = proj()
        yield
        msq = jnp.dot((qb * qb).astype(BF16), hsel_ref[0:w, 0:w], preferred_element_type=F32)
        aq_ref[rows, c:c + w] = (qb * lax.rsqrt(msq + EPS) * qn_ref[:, c:c + w]).astype(BF16)
        yield
    assert 2 * KV_W == w
    kv = proj()
    yield
    kb = kv[:, 0:KV_W]
    msk = jnp.dot((kb * kb).astype(BF16), hsel_ref[0:KV_W, 0:KV_W], preferred_element_type=F32)
    ak_ref[rows, :] = kb * lax.rsqrt(msk + EPS) * kn_ref[...]
    av_ref[rows, :] = kv[:, KV_W:2 * KV_W]
    yield

    d = x.shape[-1]
    for gate_ref in (ga_ref, gb_ref):
        for c in range(0, d, w):
            g = proj()
            yield
            gate_ref[rows, c:c + w] = _sigmoid(g).astype(BF16)
            yield


def _proj_kernel(*refs):
    half = ROW_TILE // 2
    first = _proj_rows(slice(0, half), *refs)
    second = _proj_rows(slice(half, ROW_TILE), *refs)
    next(first)
    _interleave([first, second], [1, 1])


def _proj(x, n1, win, lbl, qn, kn, hsel):
    n, d = x.shape
    pw = win.shape[1]
    tm = ROW_TILE
    row = lambda w: pl.BlockSpec((tm, w), lambda i: (i, 0))
    out_shapes = (
        jax.ShapeDtypeStruct((n, HG_W), BF16),
        jax.ShapeDtypeStruct((n, HG_W), F32),
        jax.ShapeDtypeStruct((n, HG_W), BF16),
        jax.ShapeDtypeStruct((n, HG_W), BF16),
        jax.ShapeDtypeStruct((n, Q_W), BF16),
        jax.ShapeDtypeStruct((n, KV_W), F32),
        jax.ShapeDtypeStruct((n, KV_W), F32),
        jax.ShapeDtypeStruct((n, d), BF16),
        jax.ShapeDtypeStruct((n, d), BF16),
    )
    vmem = _vmem_limit(
        resident=_nbytes((d, pw), BF16) + _nbytes((Q_W, Q_W), BF16),
        streamed=_nbytes((tm, d), F32) + sum(_nbytes((tm, s.shape[1]), s.dtype) for s in out_shapes),
        scratch=0,
        temporaries=8 * _nbytes((tm, d), F32))
    return pl.pallas_call(
        _proj_kernel,
        grid=(n // tm,),
        in_specs=[row(d), _const_spec((1, d)), _const_spec((d, pw)), _const_spec(lbl.shape),
                  _const_spec((1, Q_W)), _const_spec((1, KV_W)), _const_spec((Q_W, Q_W))],
        out_specs=[row(HG_W), row(HG_W), row(HG_W), row(HG_W), row(Q_W), row(KV_W), row(KV_W),
                   row(d), row(d)],
        out_shape=out_shapes,
        compiler_params=pltpu.CompilerParams(
            dimension_semantics=("arbitrary",), vmem_limit_bytes=vmem),
        name="proj",
    )(x, n1, win, lbl, qn, kn, hsel)


def _roll_rows(x, d):
    d = d % x.shape[0]
    return x if d == 0 else pltpu.roll(x, d, axis=0)


SUBLANES = 8


def _slabs(x):
    return [x[r * SUBLANES:(r + 1) * SUBLANES, :] for r in range(x.shape[0] // SUBLANES)]


def _pair_levels(rows):
    ti = lax.broadcasted_iota(jnp.int32, (rows, rows), 0)
    tj = lax.broadcasted_iota(jnp.int32, (rows, rows), 1)
    lv = 31 - lax.clz(ti ^ tj)
    return jnp.where(ti > tj, lv, jnp.where(ti == tj, DIAG, -1))


def _hgrn_levels(qh, fh, lvl_slabs, nlev):
    ns = qh.shape[0] // SUBLANES
    kh = 1.0 - fh
    qs, ks = _slabs(qh), _slabs(kh)
    incl, tot, rexc = _slabs(fh), _slabs(fh), [None] * ns
    t8 = lax.broadcasted_iota(jnp.int32, (SUBLANES, qh.shape[1]), 0)
    zero = jnp.zeros((SUBLANES, qh.shape[1]), F32)
    qb, kb = qh.astype(BF16), kh.astype(BF16)
    diag = _slabs(_nt_dot(qb, kb))
    p = [jnp.where(lvl_slabs[r] == DIAG, diag[r], 0.0) for r in range(ns)]
    for m in range(nlev):
        h = 1 << m
        hs = h // SUBLANES
        upper = [hs == 0 or (r // hs) % 2 == 1 for r in range(ns)]
        lower = [hs == 0 or (r // hs) % 2 == 0 for r in range(ns)]
        a = jnp.concatenate([qs[r] * incl[r] if upper[r] else zero for r in range(ns)], axis=0)
        if m == 0:
            b = kb
        else:
            b = jnp.concatenate([ks[r] * rexc[r] if lower[r] else zero for r in range(ns)],
                                axis=0).astype(BF16)
        sc = _slabs(_nt_dot(a.astype(BF16), b))
        p = [jnp.where(lvl_slabs[r] == m, sc[r], p[r]) if upper[r] else p[r] for r in range(ns)]
        yield
        if hs == 0:
            up = (t8 & h) != 0
            for r in range(ns):
                below = pltpu.roll(tot[r], h, axis=0)
                above = pltpu.roll(tot[r], SUBLANES - h, axis=0)
                incl[r] = incl[r] * jnp.where(up, below, 1.0)
                grow = jnp.where(up, 1.0, above)
                rexc[r] = grow if rexc[r] is None else rexc[r] * grow
                tot[r] = tot[r] * jnp.where(up, below, above)
        else:
            for b0 in range(0, ns, 2 * hs):
                t_lo, t_hi = tot[b0], tot[b0 + hs]
                both = t_lo * t_hi
                for r in range(b0, b0 + hs):
                    rexc[r] = rexc[r] * t_hi
                    tot[r] = both
                for r in range(b0 + hs, b0 + 2 * hs):
                    incl[r] = incl[r] * t_lo
                    tot[r] = both
    return jnp.concatenate(p, axis=0), qs, ks, incl, rexc, tot


def _hgrn_finish(o, nw, og):
    return o * lax.rsqrt(jnp.mean(o * o, axis=-1, keepdims=True) + EPS) * nw * og


def _hgrn_chunk(q, f, v, og, nw, states):
    nlev = CHUNK.bit_length() - 1
    lvl_slabs = _slabs(_pair_levels(CHUNK))
    ys, new_states = [], []
    for h0 in range(0, HG_HEADS, HEAD_GROUP):
        group = range(h0, h0 + HEAD_GROUP)
        sweeps = {hd: _hgrn_levels(q[:, hd * HG_DK:(hd + 1) * HG_DK],
                                   f[:, hd * HG_DK:(hd + 1) * HG_DK], lvl_slabs, nlev)
                  for hd in group}
        swept = {}
        for _ in range(nlev + 1):
            for hd in group:
                try:
                    next(sweeps[hd])
                except StopIteration as stop:
                    swept[hd] = stop.value
            yield
        for hd in group:
            sl = slice(hd * HG_DK, (hd + 1) * HG_DK)
            p, qs, ks, incl, rexc, tot = swept[hd]
            ns = len(qs)
            qd = jnp.concatenate([qs[r] * incl[r] for r in range(ns)], axis=0).astype(BF16)
            kd = jnp.concatenate([ks[r] * rexc[r] for r in range(ns)], axis=0).astype(BF16)
            st = states[hd]
            o = (jnp.dot(p.astype(BF16), v[:, sl], preferred_element_type=F32)
                 + _nt_dot(qd, st.astype(BF16)))
            new_states.append(st * tot[0][0:1, :] + _tn_dot(v[:, sl], kd))
            ys.append(_hgrn_finish(o, nw[:, sl], og[:, sl]))
            yield
    return ys, new_states


def _hgrn_sample_kernel(hq_ref, f_ref, hv_ref, og_ref, nw_ref, s_ref, ya_ref, so_ref, *, seq_len):
    nlev = SEQ_PAD.bit_length() - 1
    t = lax.broadcasted_iota(jnp.int32, (CHUNK, HG_W), 0)
    f = jnp.where((t & (SEQ_PAD - 1)) < seq_len, f_ref[...], 1.0)
    q = hq_ref[...].astype(F32)
    v = hv_ref[...]
    vf = v.astype(F32)
    og = og_ref[...].astype(F32)
    nw = nw_ref[...]
    lvl_slabs = _slabs(_pair_levels(CHUNK))
    assert SEQ_PAD == SUBLANES
    for hd in range(HG_HEADS):
        sl = slice(hd * HG_DK, (hd + 1) * HG_DK)
        p, qs, ks, incl, rexc, tot = _drain(_hgrn_levels(q[:, sl], f[:, sl], lvl_slabs, nlev))
        o_inter = []
        for s in range(CHUNK // SEQ_PAD):
            rows = slice(s * SEQ_PAD, (s + 1) * SEQ_PAD)
            st = s_ref[s, hd]
            o_inter.append(jnp.dot(qs[s] * incl[s], st, preferred_element_type=F32))
            decay = jnp.broadcast_to(tot[s][0:1, :], (HG_DV, HG_DK)).T
            so_ref[s, hd] = st * decay + _tn_dot(ks[s] * rexc[s], vf[rows, sl])
        o = (jnp.dot(p.astype(BF16), v[:, sl], preferred_element_type=F32)
             + jnp.concatenate(o_inter, axis=0))
        ya_ref[:, sl] = _hgrn_finish(o, nw[:, sl], og[:, sl]).astype(BF16)


def _hgrn_sample(hq, f, hv, og, nw, state, seq_len):
    n = hq.shape[0]
    spc = CHUNK // SEQ_PAD
    blk = pl.BlockSpec((CHUNK, HG_W), lambda i: (i, 0))
    sblk = pl.BlockSpec((spc, HG_HEADS, HG_DK, HG_DV), lambda i: (i, 0, 0, 0))
    state_block = _nbytes((spc, HG_HEADS, HG_DK, HG_DV), F32)
    vmem = _vmem_limit(
        resident=0,
        streamed=2 * state_block + _nbytes((CHUNK, HG_W), F32) + 4 * _nbytes((CHUNK, HG_W), BF16),
        scratch=0,
        temporaries=16 * _nbytes((CHUNK, HG_W), F32))
    return pl.pallas_call(
        functools.partial(_hgrn_sample_kernel, seq_len=seq_len),
        grid=(n // CHUNK,),
        in_specs=[blk, blk, blk, blk, _const_spec((1, HG_W)), sblk],
        out_specs=[blk, sblk],
        out_shape=(jax.ShapeDtypeStruct((n, HG_W), BF16),
                   jax.ShapeDtypeStruct(state.shape, F32)),
        compiler_params=pltpu.CompilerParams(
            dimension_semantics=("arbitrary",), vmem_limit_bytes=vmem),
        name="hgrn_sample",
    )(hq, f, hv, og, nw, state)


LOG2E = 1.4426950408889634


def _alibi_slope(head):
    return LOG2E * 2.0 ** (-8.0 * (head + 1) / N_Q_HEADS)


def _tile_kv_head(x, g):
    lane = lax.broadcasted_iota(jnp.int32, x.shape, 1)
    other = pltpu.roll(x, HEAD_DIM, axis=1)
    mine = (lane >> HEAD_SHIFT) == g
    one = jnp.where(mine, x, other).astype(BF16)
    return jnp.concatenate([one] * (GROUP_W // KV_W), axis=1)


def _attn_group(qg, kt, vt, bias_of, sink_of):
    lane_head = lax.broadcasted_iota(jnp.int32, kt.shape, 1) >> HEAD_SHIFT
    zero = jnp.zeros_like(kt)
    vbd = jnp.concatenate([jnp.where(lane_head == hh, vt, zero) for hh in range(Q_PER_KV)], axis=0)
    ps, invs = [], []
    for hh in range(Q_PER_KV):
        s = _nt_dot(qg, jnp.where(lane_head == hh, kt, zero)) + bias_of(hh)
        sink = sink_of(hh) * LOG2E
        m = jnp.maximum(jnp.max(s, axis=-1, keepdims=True), sink)
        yield
        p = jnp.exp2(s - m)
        denom = jnp.sum(p, axis=-1, keepdims=True) + jnp.exp2(sink - m)
        ps.append(p.astype(BF16))
        invs.append(1.0 / denom)
        yield
    o = jnp.dot(jnp.concatenate(ps, axis=1), vbd, preferred_element_type=F32)
    out_head = lax.broadcasted_iota(jnp.int32, o.shape, 1) >> HEAD_SHIFT
    scale = invs[Q_PER_KV - 1]
    for hh in range(Q_PER_KV - 2, -1, -1):
        scale = jnp.where(out_head == hh, invs[hh], scale)
    yield
    return o * scale


def _init_band_bias(bias_ref):
    tq = lax.broadcasted_iota(jnp.int32, (WINDOW, 2 * WINDOW), 0)
    c = lax.broadcasted_iota(jnp.int32, (WINDOW, 2 * WINDOW), 1)
    dist = WINDOW + tq - c
    valid = (dist >= 0) & (dist < WINDOW)
    for h in range(N_Q_HEADS):
        bias = jnp.where(valid, -_alibi_slope(h) * dist.astype(F32), NEG_BIG)
        bias_ref[0, h] = bias
        bias_ref[1, h] = jnp.where(c < WINDOW, NEG_BIG, bias)


def _swa_block(q, k2, v2, bias_ref, table, sink_ref):
    outs = []
    for g in range(N_KV_HEADS):
        kt = _tile_kv_head(k2, g)
        vt = _tile_kv_head(v2, g)
        o = yield from _attn_group(q[:, g * GROUP_W:(g + 1) * GROUP_W], kt, vt,
                                   lambda hh, g=g: bias_ref[table, g * Q_PER_KV + hh],
                                   lambda hh, g=g: sink_ref[g * Q_PER_KV + hh])
        outs.append(o)
    return outs


SWA_SAMPLE_SEQS = 16
SWA_SAMPLE_BATCH = 8
STACK = Q_PER_KV * SEQ_PAD


def _swa_sample_kernel(sink_ref, q_ref, kn_ref, vn_ref, ckt_ref, cvt_ref,
                       y_ref, ckot_ref, cvot_ref, bias_ref, sinkv_ref, *, seq_len):
    cw = ckt_ref.shape[2]

    @pl.when(pl.program_id(0) == 0)
    def _():
        rows = SWA_SAMPLE_BATCH * STACK
        r = lax.broadcasted_iota(jnp.int32, (rows, cw), 0)
        c = lax.broadcasted_iota(jnp.int32, (rows, cw), 1)
        tq = r & (SEQ_PAD - 1)
        hh = (r >> (SEQ_PAD.bit_length() - 1)) & (Q_PER_KV - 1)
        dist_c = cw + tq - c
        valid_c = (dist_c >= 0) & (dist_c < WINDOW)
        dist_n = tq - c
        valid_n = (dist_n >= 0) & (dist_n < WINDOW) & (c < seq_len)
        for g in range(N_KV_HEADS):
            slope = jnp.full((rows, cw), _alibi_slope(g * Q_PER_KV + Q_PER_KV - 1), F32)
            sink = jnp.full((rows, cw), sink_ref[g * Q_PER_KV + Q_PER_KV - 1] * LOG2E, F32)
            for h in range(Q_PER_KV - 2, -1, -1):
                slope = jnp.where(hh == h, _alibi_slope(g * Q_PER_KV + h), slope)
                sink = jnp.where(hh == h, sink_ref[g * Q_PER_KV + h] * LOG2E, sink)
            bias_ref[g, 0] = jnp.where(valid_c, -slope * dist_c.astype(F32), NEG_BIG)
            bias_ref[g, 1] = jnp.where(valid_n, -slope * dist_n.astype(F32), NEG_BIG)
            sinkv_ref[g] = sink

    lane = lax.broadcasted_iota(jnp.int32, (KV_W, cw), 1)
    keep = lane < cw - seq_len
    pad = jnp.zeros((cw - SEQ_PAD, KV_W), F32)

    nb = SWA_SAMPLE_BATCH
    shift = cw - seq_len

    def body(i, carry):
        seqs = [i * nb + j for j in range(nb)]
        q = [q_ref[s].astype(F32) for s in seqs]
        k_new = [jnp.concatenate([kn_ref[s], pad], axis=0) for s in seqs]
        v_new = [jnp.concatenate([vn_ref[s], pad], axis=0) for s in seqs]
        ckt = [ckt_ref[s] for s in seqs]
        cvt = [cvt_ref[s] for s in seqs]
        for g in range(N_KV_HEADS):
            ch = slice(g * HEAD_DIM, (g + 1) * HEAD_DIM)
            qs = [jnp.concatenate(
                [qj[:, (g * Q_PER_KV + h) * HEAD_DIM:(g * Q_PER_KV + h + 1) * HEAD_DIM]
                 for h in range(Q_PER_KV)], axis=0).astype(BF16) for qj in q]
            s_c = jnp.concatenate(
                [jnp.dot(qs[j], ckt[j][ch, :].astype(BF16), preferred_element_type=F32)
                 for j in range(nb)], axis=0) + bias_ref[g, 0]
            s_n = jnp.concatenate(
                [_nt_dot(qs[j], k_new[j][:, ch].astype(BF16)) for j in range(nb)],
                axis=0) + bias_ref[g, 1]
            sink = sinkv_ref[g][:, 0:1]
            m = jnp.maximum(jnp.maximum(jnp.max(s_c, axis=-1, keepdims=True),
                                        jnp.max(s_n, axis=-1, keepdims=True)), sink)
            p_c = jnp.exp2(s_c - m)
            p_n = jnp.exp2(s_n - m)
            denom = (jnp.sum(p_c, axis=-1, keepdims=True) + jnp.sum(p_n, axis=-1, keepdims=True)
                     + jnp.exp2(sink - m))
            inv = 1.0 / denom
            p_c, p_n = p_c.astype(BF16), p_n.astype(BF16)
            for j in range(nb):
                rows = slice(j * STACK, (j + 1) * STACK)
                o = (_nt_dot(p_c[rows, :], cvt[j][ch, :].astype(BF16))
                     + jnp.dot(p_n[rows, :], v_new[j][:, ch].astype(BF16),
                               preferred_element_type=F32)) * inv[rows, :]
                y = jnp.concatenate(
                    [o[h * SEQ_PAD:(h + 1) * SEQ_PAD, :] for h in range(Q_PER_KV)], axis=1)
                y_ref[seqs[j], :, g * GROUP_W:(g + 1) * GROUP_W] = y.astype(BF16)
        for j in range(nb):
            ckot_ref[seqs[j]] = jnp.where(keep, pltpu.roll(ckt[j], shift, axis=1),
                                          _roll_rows(k_new[j], shift).T)
            cvot_ref[seqs[j]] = jnp.where(keep, pltpu.roll(cvt[j], shift, axis=1),
                                          _roll_rows(v_new[j], shift).T)
        return carry

    lax.fori_loop(0, q_ref.shape[0] // nb, body, 0)


def _swa_sample(sinks, aq, ak, av, cache_kt, cache_vt, seq_len):
    nseq, _, cw = cache_kt.shape
    g = SWA_SAMPLE_SEQS
    blk = lambda r, w: pl.BlockSpec((g, r, w), lambda i: (i, 0, 0))
    rows = SWA_SAMPLE_BATCH * STACK
    table = _nbytes((N_KV_HEADS, rows, cw), F32)
    vmem = _vmem_limit(
        resident=0,
        streamed=(4 * _nbytes((g, KV_W, cw), F32) + 2 * _nbytes((g, SEQ_PAD, Q_W), BF16)
                  + 2 * _nbytes((g, SEQ_PAD, KV_W), F32)),
        scratch=3 * table,
        temporaries=(4 * SWA_SAMPLE_BATCH + 8) * _nbytes((KV_W, cw), F32))
    return pl.pallas_call(
        functools.partial(_swa_sample_kernel, seq_len=seq_len),
        grid=(nseq // g,),
        in_specs=[pl.BlockSpec(memory_space=pltpu.SMEM),
                  blk(SEQ_PAD, Q_W), blk(SEQ_PAD, KV_W), blk(SEQ_PAD, KV_W),
                  blk(KV_W, cw), blk(KV_W, cw)],
        out_specs=[blk(SEQ_PAD, Q_W), blk(KV_W, cw), blk(KV_W, cw)],
        out_shape=(jax.ShapeDtypeStruct((nseq, SEQ_PAD, Q_W), BF16),
                   jax.ShapeDtypeStruct((nseq, KV_W, cw), F32),
                   jax.ShapeDtypeStruct((nseq, KV_W, cw), F32)),
        scratch_shapes=[pltpu.VMEM((N_KV_HEADS, 2, rows, cw), F32),
                        pltpu.VMEM((N_KV_HEADS, rows, cw), F32)],
        compiler_params=pltpu.CompilerParams(
            dimension_semantics=("arbitrary",), vmem_limit_bytes=vmem),
        name="swa_sample",
    )(sinks, aq, ak, av, cache_kt, cache_vt)


def _merge_ffn_rows(x, ya, yb, ga, gb, woh_ref, woa_ref, wout_ref, n2_ref, wg_ref, wu_ref, wd_ref,
                    act_ref):
    branch_a = jnp.dot(ya, woh_ref[...], preferred_element_type=F32)
    yield
    branch_b = jnp.dot(yb, woa_ref[...], preferred_element_type=F32)
    yield
    merged = ga.astype(F32) * branch_a + gb.astype(F32) * branch_b
    x1 = x + jnp.dot(merged.astype(BF16), wout_ref[...], preferred_element_type=F32)
    yield
    h2 = (x1 * lax.rsqrt(jnp.mean(x1 * x1, axis=-1, keepdims=True) + EPS) * n2_ref[...]).astype(BF16)
    d_ff = wg_ref.shape[1]
    for c in range(d_ff // FFN_CHUNK):
        cols = slice(c * FFN_CHUNK, (c + 1) * FFN_CHUNK)
        gate = jnp.dot(h2, wg_ref[:, cols], preferred_element_type=F32)
        yield
        up = jnp.dot(h2, wu_ref[:, cols], preferred_element_type=F32)
        yield
        act_ref[:, cols] = (gate * _sigmoid(gate) * up).astype(BF16)
        yield
    act = act_ref[...]
    outs = []
    for n in range(x.shape[1] // DOWN_TILE):
        cols = slice(n * DOWN_TILE, (n + 1) * DOWN_TILE)
        outs.append(x1[:, cols] + jnp.dot(act, wd_ref[:, cols], preferred_element_type=F32))
        yield
    return jnp.concatenate(outs, axis=1)


def _merge_ffn_kernel(x_ref, ya_ref, yb_ref, ga_ref, gb_ref, woh_ref, woa_ref, wout_ref,
                      n2_ref, wg_ref, wu_ref, wd_ref, o_ref, act_s):
    o_ref[...] = _drain(_merge_ffn_rows(
        x_ref[...], ya_ref[...], yb_ref[...], ga_ref[...], gb_ref[...],
        woh_ref, woa_ref, wout_ref, n2_ref, wg_ref, wu_ref, wd_ref, act_s))


def _mixers_kernel(sink_ref, hq_ref, f_ref, hv_ref, og_ref, aq_ref, kp_ref, kc_ref, vp_ref, vc_ref,
                   nw_ref, ya_ref, yb_ref, sfin_ref, klast_ref, vlast_ref, st_ref, bias_ref, *, nb):
    s = pl.program_id(0)

    @pl.when(s == 0)
    def _():
        _init_band_bias(bias_ref)
        st_ref[...] = jnp.zeros_like(st_ref)

    pos = lax.rem(s, nb)
    first = pos == 0

    def recurrence():
        states = [jnp.where(first, 0.0, st_ref[hd]) for hd in range(HG_HEADS)]
        ys_all = []
        for sub in range(MIX_SUB):
            rows = slice(sub * CHUNK, (sub + 1) * CHUNK)
            ys, states = yield from _hgrn_chunk(
                hq_ref[rows, :].astype(F32), f_ref[rows, :], hv_ref[rows, :],
                og_ref[rows, :].astype(F32), nw_ref[...], states)
            ys_all.append(ys)
        return ys_all, states

    def attention():
        k_prev, v_prev = kp_ref[...], vp_ref[...]
        outs_all = []
        for sub in range(MIX_SUB):
            rows = slice(sub * CHUNK, (sub + 1) * CHUNK)
            k_cur, v_cur = kc_ref[rows, :], vc_ref[rows, :]
            table = jnp.where(first, 1, 0) if sub == 0 else 0
            outs = yield from _swa_block(
                aq_ref[rows, :], jnp.concatenate([k_prev, k_cur], axis=0),
                jnp.concatenate([v_prev, v_cur], axis=0), bias_ref, table, sink_ref)
            k_prev, v_prev = k_cur, v_cur
            outs_all.append(outs)
        return outs_all

    (ys_all, new_states), outs_all = _interleave(
        [recurrence(), attention()], [HGRN_PIECES, SWA_PIECES])

    for hd in range(HG_HEADS):
        st_ref[hd] = new_states[hd]
    for sub in range(MIX_SUB):
        rows = slice(sub * CHUNK, (sub + 1) * CHUNK)
        for hd in range(HG_HEADS):
            ya_ref[rows, hd * HG_DV:(hd + 1) * HG_DV] = ys_all[sub][hd].astype(BF16)
        for g in range(N_KV_HEADS):
            yb_ref[rows, g * GROUP_W:(g + 1) * GROUP_W] = outs_all[sub][g].astype(BF16)

    @pl.when(pos == nb - 1)
    def _():
        for hd in range(HG_HEADS):
            sfin_ref[0, hd] = st_ref[hd].T
        last = slice((MIX_SUB - 1) * CHUNK, MIX_SUB * CHUNK)
        klast_ref[0] = kc_ref[last, :].T
        vlast_ref[0] = vc_ref[last, :].T


def _mixers(sinks, hq, f, hv, og, aq, ak, av, nw, batch, seq):
    n = batch * seq
    rows = MIX_SUB * CHUNK
    nb = seq // rows
    cur = lambda w: pl.BlockSpec((rows, w), lambda s: (s, 0))
    prev = lambda w: pl.BlockSpec((CHUNK, w), lambda s: (jnp.maximum(MIX_SUB * s - 1, 0), 0))
    per_seq = lambda *shape: pl.BlockSpec((1,) + shape, lambda s: (s // nb,) + (0,) * len(shape))
    state = _nbytes((HG_HEADS, HG_DV, HG_DK), F32)
    bias = _nbytes((2, N_Q_HEADS, WINDOW, 2 * WINDOW), F32)
    vmem = _vmem_limit(
        resident=0,
        streamed=(_nbytes((rows, HG_W), F32) + 6 * _nbytes((rows, HG_W), BF16)
                  + 2 * _nbytes((rows + CHUNK, KV_W), F32) + state + 2 * _nbytes((KV_W, WINDOW), F32)),
        scratch=state + bias,
        temporaries=MIX_SUB * 10 * _nbytes((CHUNK, HG_W), F32))
    return pl.pallas_call(
        functools.partial(_mixers_kernel, nb=nb),
        grid=(batch * nb,),
        in_specs=[pl.BlockSpec(memory_space=pltpu.SMEM),
                  cur(HG_W), cur(HG_W), cur(HG_W), cur(HG_W),
                  cur(Q_W), prev(KV_W), cur(KV_W), prev(KV_W), cur(KV_W),
                  _const_spec((1, HG_W))],
        out_specs=[cur(HG_W), cur(Q_W), per_seq(HG_HEADS, HG_DK, HG_DV),
                   per_seq(KV_W, WINDOW), per_seq(KV_W, WINDOW)],
        out_shape=(jax.ShapeDtypeStruct((n, HG_W), BF16),
                   jax.ShapeDtypeStruct((n, Q_W), BF16),
                   jax.ShapeDtypeStruct((batch, HG_HEADS, HG_DK, HG_DV), F32),
                   jax.ShapeDtypeStruct((batch, KV_W, WINDOW), F32),
                   jax.ShapeDtypeStruct((batch, KV_W, WINDOW), F32)),
        scratch_shapes=[pltpu.VMEM((HG_HEADS, HG_DV, HG_DK), F32),
                        pltpu.VMEM((2, N_Q_HEADS, WINDOW, 2 * WINDOW), F32)],
        compiler_params=pltpu.CompilerParams(
            dimension_semantics=("arbitrary",), vmem_limit_bytes=vmem),
        name="mixers",
    )(sinks, hq, f, hv, og, aq, ak, ak, av, av, nw)


def _merge_ffn(x, ya, yb, ga, gb, woh, woa, wout, n2, wg, wu, wd):
    n, d = x.shape
    d_ff = wg.shape[1]
    tm = ROW_TILE
    row = lambda w: pl.BlockSpec((tm, w), lambda i: (i, 0))
    vmem = _vmem_limit(
        resident=_nbytes((HG_W + Q_W + d, d), BF16) + 3 * _nbytes((d, d_ff), BF16),
        streamed=2 * _nbytes((tm, d), F32) + _nbytes((tm, HG_W + Q_W + 2 * d), BF16),
        scratch=_nbytes((tm, d_ff), BF16),
        temporaries=5 * _nbytes((tm, d), F32))
    return pl.pallas_call(
        _merge_ffn_kernel,
        grid=(n // tm,),
        in_specs=[row(d), row(HG_W), row(Q_W), row(d), row(d),
                  _const_spec((HG_W, d)), _const_spec((Q_W, d)), _const_spec((d, d)),
                  _const_spec((1, d)), _const_spec((d, d_ff)), _const_spec((d, d_ff)),
                  _const_spec((d_ff, d))],
        out_specs=row(d),
        out_shape=jax.ShapeDtypeStruct((n, d), F32),
        scratch_shapes=[pltpu.VMEM((tm, d_ff), BF16)],
        compiler_params=pltpu.CompilerParams(
            dimension_semantics=("arbitrary",), vmem_limit_bytes=vmem),
        name="merge_ffn",
    )(x, ya, yb, ga, gb, woh, woa, wout, n2, wg, wu, wd)


def kernel(x_prompt, x_sample, state_hgrn, cache_k, cache_v, norm1_w, w_in, lb_logits, hgrn_norm_w,
           q_norm_w, k_norm_w, sinks, w_o_hgrn, w_o_attn, w_out, norm2_w, w_ffn_gate, w_ffn_up,
           w_ffn_down):
    depth = w_in.shape[0]
    assert depth == 1 and lb_logits.shape[0] == 2, "single-layer step only"
    batch, seq, d = x_prompt.shape
    nseq, dec_seq, _ = x_sample.shape
    cw = cache_k.shape[2]
    assert seq % ROW_TILE == 0 and seq % CHUNK == 0 and seq >= WINDOW
    assert dec_seq <= SEQ_PAD and cw == WINDOW and w_ffn_gate.shape[2] % FFN_CHUNK == 0
    assert (nseq * SEQ_PAD) % ROW_TILE == 0 and nseq % SWA_SAMPLE_SEQS == 0

    win = w_in[0].astype(BF16)
    n1 = norm1_w[0][None, :]
    n2 = norm2_w[0][None, :]
    qn = jnp.tile(q_norm_w[0], N_Q_HEADS)[None, :] * (HEAD_DIM ** -0.5 * LOG2E)
    kn = jnp.tile(k_norm_w[0], N_KV_HEADS)[None, :]
    nw = hgrn_norm_w[0][None, :]
    head_of = jnp.arange(Q_W, dtype=jnp.int32) // HEAD_DIM
    hsel = jnp.where(head_of[:, None] == head_of[None, :], 1.0 / HEAD_DIM, 0.0).astype(BF16)
    sk = sinks[0].astype(F32)
    woh, woa, wout = w_o_hgrn[0].astype(BF16), w_o_attn[0].astype(BF16), w_out[0].astype(BF16)
    wg, wu, wd = w_ffn_gate[0].astype(BF16), w_ffn_up[0].astype(BF16), w_ffn_down[0].astype(BF16)

    xp = x_prompt.reshape(batch * seq, d)
    hq, f, hv, og, aq, ak, av, ga, gb = _proj(xp, n1, win, lb_logits, qn, kn, hsel)
    ya, yb, s_prompt, kt_prompt, vt_prompt = _mixers(sk, hq, f, hv, og, aq, ak, av, nw, batch, seq)
    y_prompt = _merge_ffn(xp, ya, yb, ga, gb, woh, woa, wout, n2, wg, wu, wd).reshape(batch, seq, d)

    def from_transposed(t):
        return t.reshape(t.shape[0], N_KV_HEADS, HEAD_DIM, t.shape[2]).transpose(0, 3, 1, 2)

    def to_transposed(c):
        return c.transpose(0, 2, 3, 1).reshape(c.shape[0], KV_W, c.shape[1])

    xs = jnp.pad(x_sample, ((0, 0), (0, SEQ_PAD - dec_seq), (0, 0))).reshape(nseq * SEQ_PAD, d)
    hq, f, hv, og, aq, ak, av, ga, gb = _proj(xs, n1, win, lb_logits, qn, kn, hsel)
    ya, s_sample = _hgrn_sample(hq, f, hv, og, nw, state_hgrn[0], dec_seq)
    yb, kt_sample, vt_sample = _swa_sample(
        sk, aq.reshape(nseq, SEQ_PAD, Q_W), ak.reshape(nseq, SEQ_PAD, KV_W),
        av.reshape(nseq, SEQ_PAD, KV_W), to_transposed(cache_k[0]), to_transposed(cache_v[0]),
        dec_seq)
    ys = _merge_ffn(xs, ya, yb.reshape(nseq * SEQ_PAD, Q_W), ga, gb, woh, woa, wout, n2, wg, wu, wd)
    y_sample = ys.reshape(nseq, SEQ_PAD, d)[:, :dec_seq]

    return (y_prompt, y_sample, s_prompt[None], from_transposed(kt_prompt)[None],
            from_transposed(vt_prompt)[None], s_sample[None], from_transposed(kt_sample)[None],
            from_transposed(vt_sample)[None])
```

```python
import functools
import math

import jax
import jax.numpy as jnp
from jax import lax
from jax.experimental import pallas as pl
from jax.experimental.pallas import tpu as pltpu

F32 = jnp.float32
BF16 = jnp.bfloat16

EPS = 1e-6
NEG_BIG = -1e30

HG_HEADS = 4
HG_DK = 128
HG_DV = 128
HG_W = HG_HEADS * HG_DK
N_Q_HEADS = 8
N_KV_HEADS = 2
Q_PER_KV = N_Q_HEADS // N_KV_HEADS
HEAD_DIM = 64
HEAD_SHIFT = HEAD_DIM.bit_length() - 1
WINDOW = 128
Q_W = N_Q_HEADS * HEAD_DIM
KV_W = N_KV_HEADS * HEAD_DIM
GROUP_W = Q_PER_KV * HEAD_DIM

SEQ_PAD = 8
ROW_TILE = 512
CHUNK = 128
FFN_CHUNK = 256
DOWN_TILE = 1024
MIX_SUB = 4
PROJ_TILE = 256
HEAD_GROUP = 2
HGRN_PIECES = MIX_SUB * 20
SWA_PIECES = MIX_SUB * 18
DIAG = 64
V7X_VMEM_BYTES = 64 * 1024 * 1024
SCOPED_VMEM_BYTES = V7X_VMEM_BYTES - 4 * 1024 * 1024


def _nbytes(shape, dtype):
    return math.prod(shape) * jnp.dtype(dtype).itemsize


def _vmem_limit(resident, streamed, scratch, temporaries):
    need = resident + 2 * streamed + scratch + temporaries
    assert need <= SCOPED_VMEM_BYTES, need
    return SCOPED_VMEM_BYTES


def _nt_dot(a, b):
    return lax.dot_general(a, b, (((1,), (1,)), ((), ())), preferred_element_type=F32)


def _tn_dot(a, b):
    return lax.dot_general(a, b, (((0,), (0,)), ((), ())), preferred_element_type=F32)


def _sigmoid(x):
    return 1.0 / (1.0 + jnp.exp(-x))


def _drain(gen):
    try:
        while True:
            next(gen)
    except StopIteration as stop:
        return stop.value


def _interleave(gens, pieces):
    done = [0] * len(gens)
    results = [None] * len(gens)
    live = list(range(len(gens)))
    while live:
        i = min(live, key=lambda g: ((done[g] + 1) / pieces[g], g))
        try:
            next(gens[i])
            done[i] += 1
        except StopIteration as stop:
            results[i] = stop.value
            live.remove(i)
    return results


def _const_spec(shape):
    nd = len(shape)
    return pl.BlockSpec(shape, lambda *_: (0,) * nd, pipeline_mode=pl.Buffered(1))


def _proj_rows(rows, x_ref, n1_ref, win_ref, lbl_ref, qn_ref, kn_ref, hsel_ref,
               hq_ref, f_ref, hv_ref, og_ref, aq_ref, ak_ref, av_ref, ga_ref, gb_ref):
    x = x_ref[rows, :]
    h = x * lax.rsqrt(jnp.mean(x * x, axis=-1, keepdims=True) + EPS) * n1_ref[...]
    hb = h.astype(BF16)
    yield

    w = PROJ_TILE
    col = [0]

    def proj():
        res = jnp.dot(hb, win_ref[:, col[0]:col[0] + w], preferred_element_type=F32)
        col[0] += w
        return res

    lg = lbl_ref[...]
    e = jnp.exp(lg - jnp.max(lg, axis=0, keepdims=True))
    lb = e[0:1, :] / jnp.sum(e, axis=0, keepdims=True)

    for c in range(0, HG_W, w):
        qa = proj()
        yield
        hq_ref[rows, c:c + w] = (qa * _sigmoid(qa) * (HG_DK ** -0.5)).astype(BF16)
        yield
    for c in range(0, HG_W, w):
        fa = proj()
        yield
        f_ref[rows, c:c + w] = lb[:, c:c + w] + (1.0 - lb[:, c:c + w]) * _sigmoid(fa)
        yield
    for c in range(0, HG_W, w):
        hv_ref[rows, c:c + w] = proj().astype(BF16)
        yield
    for c in range(0, HG_W, w):
        go = proj()
        yield
        og_ref[rows, c:c + w] = (go * _sigmoid(go)).astype(BF16)
        yield

    for c in range(0, Q_W, w):
        qb = proj()
        yield
        msq = jnp.dot((qb * qb).astype(BF16), hsel_ref[0:w, 0:w], preferred_element_type=F32)
        aq_ref[rows, c:c + w] = (qb * lax.rsqrt(msq + EPS) * qn_ref[:, c:c + w]).astype(BF16)
        yield
    assert 2 * KV_W == w
    kv = proj()
    yield
    kb = kv[:, 0:KV_W]
    msk = jnp.dot((kb * kb).astype(BF16), hsel_ref[0:KV_W, 0:KV_W], preferred_element_type=F32)
    ak_ref[rows, :] = kb * lax.rsqrt(msk + EPS) * kn_ref[...]
    av_ref[rows, :] = kv[:, KV_W:2 * KV_W]
    yield

    d = x.shape[-1]
    for gate_ref in (ga_ref, gb_ref):
        for c in range(0, d, w):
            g = proj()
            yield
            gate_ref[rows, c:c + w] = _sigmoid(g).astype(BF16)
            yield


def _proj_kernel(*refs):
    half = ROW_TILE // 2
    first = _proj_rows(slice(0, half), *refs)
    second = _proj_rows(slice(half, ROW_TILE), *refs)
    next(first)
    _interleave([first, second], [1, 1])


def _proj(x, n1, win, lbl, qn, kn, hsel):
    n, d = x.shape
    pw = win.shape[1]
    tm = ROW_TILE
    row = lambda w: pl.BlockSpec((tm, w), lambda i: (i, 0))
    out_shapes = (
        jax.ShapeDtypeStruct((n, HG_W), BF16),
        jax.ShapeDtypeStruct((n, HG_W), F32),
        jax.ShapeDtypeStruct((n, HG_W), BF16),
        jax.ShapeDtypeStruct((n, HG_W), BF16),
        jax.ShapeDtypeStruct((n, Q_W), BF16),
        jax.ShapeDtypeStruct((n, KV_W), F32),
        jax.ShapeDtypeStruct((n, KV_W), F32),
        jax.ShapeDtypeStruct((n, d), BF16),
        jax.ShapeDtypeStruct((n, d), BF16),
    )
    vmem = _vmem_limit(
        resident=_nbytes((d, pw), BF16) + _nbytes((Q_W, Q_W), BF16),
        streamed=_nbytes((tm, d), F32) + sum(_nbytes((tm, s.shape[1]), s.dtype) for s in out_shapes),
        scratch=0,
        temporaries=3 * _nbytes((tm, d), F32))
    return pl.pallas_call(
        _proj_kernel,
        grid=(n // tm,),
        in_specs=[row(d), _const_spec((1, d)), _const_spec((d, pw)), _const_spec(lbl.shape),
                  _const_spec((1, Q_W)), _const_spec((1, KV_W)), _const_spec((Q_W, Q_W))],
        out_specs=[row(HG_W), row(HG_W), row(HG_W), row(HG_W), row(Q_W), row(KV_W), row(KV_W),
                   row(d), row(d)],
        out_shape=out_shapes,
        compiler_params=pltpu.CompilerParams(
            dimension_semantics=("arbitrary",), vmem_limit_bytes=vmem),
        name="proj",
    )(x, n1, win, lbl, qn, kn, hsel)


def _roll_rows(x, d):
    d = d % x.shape[0]
    return x if d == 0 else pltpu.roll(x, d, axis=0)


SUBLANES = 8


def _slabs(x):
    return [x[r * SUBLANES:(r + 1) * SUBLANES, :] for r in range(x.shape[0] // SUBLANES)]


def _pair_levels(rows):
    ti = lax.broadcasted_iota(jnp.int32, (rows, rows), 0)
    tj = lax.broadcasted_iota(jnp.int32, (rows, rows), 1)
    lv = 31 - lax.clz(ti ^ tj)
    return jnp.where(ti > tj, lv, jnp.where(ti == tj, DIAG, -1))


def _hgrn_levels(qh, fh, lvl_slabs, nlev):
    ns = qh.shape[0] // SUBLANES
    kh = 1.0 - fh
    qs, ks = _slabs(qh), _slabs(kh)
    incl, tot, rexc = _slabs(fh), _slabs(fh), [None] * ns
    t8 = lax.broadcasted_iota(jnp.int32, (SUBLANES, qh.shape[1]), 0)
    zero = jnp.zeros((SUBLANES, qh.shape[1]), F32)
    qb, kb = qh.astype(BF16), kh.astype(BF16)
    diag = _slabs(_nt_dot(qb, kb))
    p = [jnp.where(lvl_slabs[r] == DIAG, diag[r], 0.0) for r in range(ns)]
    for m in range(nlev):
        h = 1 << m
        hs = h // SUBLANES
        upper = [hs == 0 or (r // hs) % 2 == 1 for r in range(ns)]
        lower = [hs == 0 or (r // hs) % 2 == 0 for r in range(ns)]
        a = jnp.concatenate([qs[r] * incl[r] if upper[r] else zero for r in range(ns)], axis=0)
        if m == 0:
            b = kb
        else:
            b = jnp.concatenate([ks[r] * rexc[r] if lower[r] else zero for r in range(ns)],
                                axis=0).astype(BF16)
        sc = _slabs(_nt_dot(a.astype(BF16), b))
        p = [jnp.where(lvl_slabs[r] == m, sc[r], p[r]) if upper[r] else p[r] for r in range(ns)]
        yield
        if hs == 0:
            up = (t8 & h) != 0
            for r in range(ns):
                below = pltpu.roll(tot[r], h, axis=0)
                above = pltpu.roll(tot[r], SUBLANES - h, axis=0)
                incl[r] = incl[r] * jnp.where(up, below, 1.0)
                grow = jnp.where(up, 1.0, above)
                rexc[r] = grow if rexc[r] is None else rexc[r] * grow
                tot[r] = tot[r] * jnp.where(up, below, above)
        else:
            for b0 in range(0, ns, 2 * hs):
                t_lo, t_hi = tot[b0], tot[b0 + hs]
                both = t_lo * t_hi
                for r in range(b0, b0 + hs):
                    rexc[r] = rexc[r] * t_hi
                    tot[r] = both
                for r in range(b0 + hs, b0 + 2 * hs):
                    incl[r] = incl[r] * t_lo
                    tot[r] = both
    return jnp.concatenate(p, axis=0), qs, ks, incl, rexc, tot


def _hgrn_finish(o, nw, og):
    return o * lax.rsqrt(jnp.mean(o * o, axis=-1, keepdims=True) + EPS) * nw * og


def _hgrn_chunk(q, f, v, og, nw, states):
    nlev = CHUNK.bit_length() - 1
    lvl_slabs = _slabs(_pair_levels(CHUNK))
    ys, new_states = [], []
    for h0 in range(0, HG_HEADS, HEAD_GROUP):
        group = range(h0, h0 + HEAD_GROUP)
        sweeps = {hd: _hgrn_levels(q[:, hd * HG_DK:(hd + 1) * HG_DK],
                                   f[:, hd * HG_DK:(hd + 1) * HG_DK], lvl_slabs, nlev)
                  for hd in group}
        swept = {}
        for _ in range(nlev + 1):
            for hd in group:
                try:
                    next(sweeps[hd])
                except StopIteration as stop:
                    swept[hd] = stop.value
            yield
        for hd in group:
            sl = slice(hd * HG_DK, (hd + 1) * HG_DK)
            p, qs, ks, incl, rexc, tot = swept[hd]
            ns = len(qs)
            qd = jnp.concatenate([qs[r] * incl[r] for r in range(ns)], axis=0).astype(BF16)
            kd = jnp.concatenate([ks[r] * rexc[r] for r in range(ns)], axis=0).astype(BF16)
            st = states[hd]
            o = (jnp.dot(p.astype(BF16), v[:, sl], preferred_element_type=F32)
                 + _nt_dot(qd, st.astype(BF16)))
            new_states.append(st * tot[0][0:1, :] + _tn_dot(v[:, sl], kd))
            ys.append(_hgrn_finish(o, nw[:, sl], og[:, sl]))
            yield
    return ys, new_states


def _hgrn_sample_kernel(hq_ref, f_ref, hv_ref, og_ref, nw_ref, s_ref, ya_ref, so_ref, *, seq_len):
    nlev = SEQ_PAD.bit_length() - 1
    t = lax.broadcasted_iota(jnp.int32, (CHUNK, HG_W), 0)
    f = jnp.where((t & (SEQ_PAD - 1)) < seq_len, f_ref[...], 1.0)
    q = hq_ref[...].astype(F32)
    v = hv_ref[...]
    vf = v.astype(F32)
    og = og_ref[...].astype(F32)
    nw = nw_ref[...]
    lvl_slabs = _slabs(_pair_levels(CHUNK))
    assert SEQ_PAD == SUBLANES
    for hd in range(HG_HEADS):
        sl = slice(hd * HG_DK, (hd + 1) * HG_DK)
        p, qs, ks, incl, rexc, tot = _drain(_hgrn_levels(q[:, sl], f[:, sl], lvl_slabs, nlev))
        o_inter = []
        for s in range(CHUNK // SEQ_PAD):
            rows = slice(s * SEQ_PAD, (s + 1) * SEQ_PAD)
            st = s_ref[s, hd]
            o_inter.append(jnp.dot(qs[s] * incl[s], st, preferred_element_type=F32))
            decay = jnp.broadcast_to(tot[s][0:1, :], (HG_DV, HG_DK)).T
            so_ref[s, hd] = st * decay + _tn_dot(ks[s] * rexc[s], vf[rows, sl])
        o = (jnp.dot(p.astype(BF16), v[:, sl], preferred_element_type=F32)
             + jnp.concatenate(o_inter, axis=0))
        ya_ref[:, sl] = _hgrn_finish(o, nw[:, sl], og[:, sl]).astype(BF16)


def _hgrn_sample(hq, f, hv, og, nw, state, seq_len):
    n = hq.shape[0]
    spc = CHUNK // SEQ_PAD
    blk = pl.BlockSpec((CHUNK, HG_W), lambda i: (i, 0))
    sblk = pl.BlockSpec((spc, HG_HEADS, HG_DK, HG_DV), lambda i: (i, 0, 0, 0))
    state_block = _nbytes((spc, HG_HEADS, HG_DK, HG_DV), F32)
    vmem = _vmem_limit(
        resident=0,
        streamed=2 * state_block + _nbytes((CHUNK, HG_W), F32) + 4 * _nbytes((CHUNK, HG_W), BF16),
        scratch=0,
        temporaries=16 * _nbytes((CHUNK, HG_W), F32))
    return pl.pallas_call(
        functools.partial(_hgrn_sample_kernel, seq_len=seq_len),
        grid=(n // CHUNK,),
        in_specs=[blk, blk, blk, blk, _const_spec((1, HG_W)), sblk],
        out_specs=[blk, sblk],
        out_shape=(jax.ShapeDtypeStruct((n, HG_W), BF16),
                   jax.ShapeDtypeStruct(state.shape, F32)),
        compiler_params=pltpu.CompilerParams(
            dimension_semantics=("arbitrary",), vmem_limit_bytes=vmem),
        name="hgrn_sample",
    )(hq, f, hv, og, nw, state)


LOG2E = 1.4426950408889634


def _alibi_slope(head):
    return LOG2E * 2.0 ** (-8.0 * (head + 1) / N_Q_HEADS)


def _tile_kv_head(x, g):
    lane = lax.broadcasted_iota(jnp.int32, x.shape, 1)
    other = pltpu.roll(x, HEAD_DIM, axis=1)
    mine = (lane >> HEAD_SHIFT) == g
    one = jnp.where(mine, x, other).astype(BF16)
    return jnp.concatenate([one] * (GROUP_W // KV_W), axis=1)


def _attn_group(qg, kt, vt, bias_of, sink_of):
    lane_head = lax.broadcasted_iota(jnp.int32, kt.shape, 1) >> HEAD_SHIFT
    zero = jnp.zeros_like(kt)
    vbd = jnp.concatenate([jnp.where(lane_head == hh, vt, zero) for hh in range(Q_PER_KV)], axis=0)
    tq = lax.broadcasted_iota(jnp.int32, (WINDOW, WINDOW), 0)
    tk = lax.broadcasted_iota(jnp.int32, (WINDOW, WINDOW), 1)
    in_cur = tk <= tq
    ps, invs = [], []
    for hh in range(Q_PER_KV):
        s2 = _nt_dot(qg, jnp.where(lane_head == hh, kt, zero))
        s = jnp.where(in_cur, s2[:, WINDOW:], s2[:, :WINDOW]) + bias_of(hh)
        sink = sink_of(hh) * LOG2E
        m = jnp.maximum(jnp.max(s, axis=-1, keepdims=True), sink)
        yield
        p = jnp.exp2(s - m)
        denom = jnp.sum(p, axis=-1, keepdims=True) + jnp.exp2(sink - m)
        ps.append(jnp.where(in_cur, 0.0, p).astype(BF16))
        ps.append(jnp.where(in_cur, p, 0.0).astype(BF16))
        invs.append(1.0 / denom)
        yield
    o = jnp.dot(jnp.concatenate(ps, axis=1), vbd, preferred_element_type=F32)
    out_head = lax.broadcasted_iota(jnp.int32, o.shape, 1) >> HEAD_SHIFT
    scale = invs[Q_PER_KV - 1]
    for hh in range(Q_PER_KV - 2, -1, -1):
        scale = jnp.where(out_head == hh, invs[hh], scale)
    yield
    return o * scale


def _init_band_bias(bias_ref):
    tq = lax.broadcasted_iota(jnp.int32, (WINDOW, WINDOW), 0)
    c = lax.broadcasted_iota(jnp.int32, (WINDOW, WINDOW), 1)
    in_cur = c <= tq
    dist = jnp.where(in_cur, tq - c, WINDOW + tq - c)
    for h in range(N_Q_HEADS):
        bias = -_alibi_slope(h) * dist.astype(F32)
        bias_ref[0, h] = bias
        bias_ref[1, h] = jnp.where(in_cur, bias, NEG_BIG)


def _swa_block(q, k2, v2, bias_ref, table, sink_ref):
    outs = []
    for g in range(N_KV_HEADS):
        kt = _tile_kv_head(k2, g)
        vt = _tile_kv_head(v2, g)
        o = yield from _attn_group(q[:, g * GROUP_W:(g + 1) * GROUP_W], kt, vt,
                                   lambda hh, g=g: bias_ref[table, g * Q_PER_KV + hh],
                                   lambda hh, g=g: sink_ref[g * Q_PER_KV + hh])
        outs.append(o)
    return outs


SWA_SAMPLE_SEQS = 16
SWA_SAMPLE_BATCH = 8
STACK = Q_PER_KV * SEQ_PAD


def _swa_sample_kernel(sink_ref, q_ref, kn_ref, vn_ref, ckt_ref, cvt_ref,
                       y_ref, ckot_ref, cvot_ref, bias_ref, sinkv_ref, *, seq_len):
    cw = ckt_ref.shape[2]

    @pl.when(pl.program_id(0) == 0)
    def _():
        rows = SWA_SAMPLE_BATCH * STACK
        r = lax.broadcasted_iota(jnp.int32, (rows, cw), 0)
        c = lax.broadcasted_iota(jnp.int32, (rows, cw), 1)
        tq = r & (SEQ_PAD - 1)
        hh = (r >> (SEQ_PAD.bit_length() - 1)) & (Q_PER_KV - 1)
        dist_c = cw + tq - c
        valid_c = (dist_c >= 0) & (dist_c < WINDOW)
        dist_n = tq - c
        valid_n = (dist_n >= 0) & (dist_n < WINDOW) & (c < seq_len)
        for g in range(N_KV_HEADS):
            slope = jnp.full((rows, cw), _alibi_slope(g * Q_PER_KV + Q_PER_KV - 1), F32)
            sink = jnp.full((rows, cw), sink_ref[g * Q_PER_KV + Q_PER_KV - 1] * LOG2E, F32)
            for h in range(Q_PER_KV - 2, -1, -1):
                slope = jnp.where(hh == h, _alibi_slope(g * Q_PER_KV + h), slope)
                sink = jnp.where(hh == h, sink_ref[g * Q_PER_KV + h] * LOG2E, sink)
            bias_ref[g, 0] = jnp.where(valid_c, -slope * dist_c.astype(F32), NEG_BIG)
            bias_ref[g, 1] = jnp.where(valid_n, -slope * dist_n.astype(F32), NEG_BIG)
            sinkv_ref[g] = sink

    lane = lax.broadcasted_iota(jnp.int32, (KV_W, cw), 1)
    keep = lane < cw - seq_len
    pad = jnp.zeros((cw - SEQ_PAD, KV_W), F32)

    nb = SWA_SAMPLE_BATCH
    shift = cw - seq_len

    def body(i, carry):
        seqs = [i * nb + j for j in range(nb)]
        q = [q_ref[s].astype(F32) for s in seqs]
        k_new = [jnp.concatenate([kn_ref[s], pad], axis=0) for s in seqs]
        v_new = [jnp.concatenate([vn_ref[s], pad], axis=0) for s in seqs]
        ckt = [ckt_ref[s] for s in seqs]
        cvt = [cvt_ref[s] for s in seqs]
        for g in range(N_KV_HEADS):
            ch = slice(g * HEAD_DIM, (g + 1) * HEAD_DIM)
            qs = [jnp.concatenate(
                [qj[:, (g * Q_PER_KV + h) * HEAD_DIM:(g * Q_PER_KV + h + 1) * HEAD_DIM]
                 for h in range(Q_PER_KV)], axis=0).astype(BF16) for qj in q]
            s_c = jnp.concatenate(
                [jnp.dot(qs[j], ckt[j][ch, :].astype(BF16), preferred_element_type=F32)
                 for j in range(nb)], axis=0) + bias_ref[g, 0]
            s_n = jnp.concatenate(
                [_nt_dot(qs[j], k_new[j][:, ch].astype(BF16)) for j in range(nb)],
                axis=0) + bias_ref[g, 1]
            sink = sinkv_ref[g][:, 0:1]
            m = jnp.maximum(jnp.maximum(jnp.max(s_c, axis=-1, keepdims=True),
                                        jnp.max(s_n, axis=-1, keepdims=True)), sink)
            p_c = jnp.exp2(s_c - m)
            p_n = jnp.exp2(s_n - m)
            denom = (jnp.sum(p_c, axis=-1, keepdims=True) + jnp.sum(p_n, axis=-1, keepdims=True)
                     + jnp.exp2(sink - m))
            inv = 1.0 / denom
            p_c, p_n = p_c.astype(BF16), p_n.astype(BF16)
            for j in range(nb):
                rows = slice(j * STACK, (j + 1) * STACK)
                o = (_nt_dot(p_c[rows, :], cvt[j][ch, :].astype(BF16))
                     + jnp.dot(p_n[rows, :], v_new[j][:, ch].astype(BF16),
                               preferred_element_type=F32)) * inv[rows, :]
                y = jnp.concatenate(
                    [o[h * SEQ_PAD:(h + 1) * SEQ_PAD, :] for h in range(Q_PER_KV)], axis=1)
                y_ref[seqs[j], :, g * GROUP_W:(g + 1) * GROUP_W] = y.astype(BF16)
        for j in range(nb):
            ckot_ref[seqs[j]] = jnp.where(keep, pltpu.roll(ckt[j], shift, axis=1),
                                          _roll_rows(k_new[j], shift).T)
            cvot_ref[seqs[j]] = jnp.where(keep, pltpu.roll(cvt[j], shift, axis=1),
                                          _roll_rows(v_new[j], shift).T)
        return carry

    lax.fori_loop(0, q_ref.shape[0] // nb, body, 0)


def _swa_sample(sinks, aq, ak, av, cache_kt, cache_vt, seq_len):
    nseq, _, cw = cache_kt.shape
    g = SWA_SAMPLE_SEQS
    blk = lambda r, w: pl.BlockSpec((g, r, w), lambda i: (i, 0, 0))
    rows = SWA_SAMPLE_BATCH * STACK
    table = _nbytes((N_KV_HEADS, rows, cw), F32)
    vmem = _vmem_limit(
        resident=0,
        streamed=(4 * _nbytes((g, KV_W, cw), F32) + 2 * _nbytes((g, SEQ_PAD, Q_W), BF16)
                  + 2 * _nbytes((g, SEQ_PAD, KV_W), F32)),
        scratch=3 * table,
        temporaries=(4 * SWA_SAMPLE_BATCH + 8) * _nbytes((KV_W, cw), F32))
    return pl.pallas_call(
        functools.partial(_swa_sample_kernel, seq_len=seq_len),
        grid=(nseq // g,),
        in_specs=[pl.BlockSpec(memory_space=pltpu.SMEM),
                  blk(SEQ_PAD, Q_W), blk(SEQ_PAD, KV_W), blk(SEQ_PAD, KV_W),
                  blk(KV_W, cw), blk(KV_W, cw)],
        out_specs=[blk(SEQ_PAD, Q_W), blk(KV_W, cw), blk(KV_W, cw)],
        out_shape=(jax.ShapeDtypeStruct((nseq, SEQ_PAD, Q_W), BF16),
                   jax.ShapeDtypeStruct((nseq, KV_W, cw), F32),
                   jax.ShapeDtypeStruct((nseq, KV_W, cw), F32)),
        scratch_shapes=[pltpu.VMEM((N_KV_HEADS, 2, rows, cw), F32),
                        pltpu.VMEM((N_KV_HEADS, rows, cw), F32)],
        compiler_params=pltpu.CompilerParams(
            dimension_semantics=("arbitrary",), vmem_limit_bytes=vmem),
        name="swa_sample",
    )(sinks, aq, ak, av, cache_kt, cache_vt)


def _merge_ffn_rows(x, ya, yb, ga, gb, woh_ref, woa_ref, wout_ref, n2_ref, wg_ref, wu_ref, wd_ref,
                    act_ref):
    branch_a = jnp.dot(ya, woh_ref[...], preferred_element_type=F32)
    yield
    branch_b = jnp.dot(yb, woa_ref[...], preferred_element_type=F32)
    yield
    merged = ga.astype(F32) * branch_a + gb.astype(F32) * branch_b
    x1 = x + jnp.dot(merged.astype(BF16), wout_ref[...], preferred_element_type=F32)
    yield
    h2 = (x1 * lax.rsqrt(jnp.mean(x1 * x1, axis=-1, keepdims=True) + EPS) * n2_ref[...]).astype(BF16)
    d_ff = wg_ref.shape[1]
    for c in range(d_ff // FFN_CHUNK):
        cols = slice(c * FFN_CHUNK, (c + 1) * FFN_CHUNK)
        gate = jnp.dot(h2, wg_ref[:, cols], preferred_element_type=F32)
        yield
        up = jnp.dot(h2, wu_ref[:, cols], preferred_element_type=F32)
        yield
        act_ref[:, cols] = (gate * _sigmoid(gate) * up).astype(BF16)
        yield
    act = act_ref[...]
    outs = []
    for n in range(x.shape[1] // DOWN_TILE):
        cols = slice(n * DOWN_TILE, (n + 1) * DOWN_TILE)
        outs.append(x1[:, cols] + jnp.dot(act, wd_ref[:, cols], preferred_element_type=F32))
        yield
    return jnp.concatenate(outs, axis=1)


def _merge_ffn_kernel(x_ref, ya_ref, yb_ref, ga_ref, gb_ref, woh_ref, woa_ref, wout_ref,
                      n2_ref, wg_ref, wu_ref, wd_ref, o_ref, act_s):
    o_ref[...] = _drain(_merge_ffn_rows(
        x_ref[...], ya_ref[...], yb_ref[...], ga_ref[...], gb_ref[...],
        woh_ref, woa_ref, wout_ref, n2_ref, wg_ref, wu_ref, wd_ref, act_s))


def _mixers_kernel(sink_ref, hq_ref, f_ref, hv_ref, og_ref, aq_ref, kp_ref, kc_ref, vp_ref, vc_ref,
                   nw_ref, ya_ref, yb_ref, sfin_ref, klast_ref, vlast_ref, st_ref, bias_ref, *, nb):
    s = pl.program_id(0)

    @pl.when(s == 0)
    def _():
        _init_band_bias(bias_ref)
        st_ref[...] = jnp.zeros_like(st_ref)

    pos = lax.rem(s, nb)
    first = pos == 0

    def recurrence():
        states = [jnp.where(first, 0.0, st_ref[hd]) for hd in range(HG_HEADS)]
        ys_all = []
        for sub in range(MIX_SUB):
            rows = slice(sub * CHUNK, (sub + 1) * CHUNK)
            ys, states = yield from _hgrn_chunk(
                hq_ref[rows, :].astype(F32), f_ref[rows, :], hv_ref[rows, :],
                og_ref[rows, :].astype(F32), nw_ref[...], states)
            ys_all.append(ys)
        return ys_all, states

    def attention():
        k_prev, v_prev = kp_ref[...], vp_ref[...]
        outs_all = []
        for sub in range(MIX_SUB):
            rows = slice(sub * CHUNK, (sub + 1) * CHUNK)
            k_cur, v_cur = kc_ref[rows, :], vc_ref[rows, :]
            table = jnp.where(first, 1, 0) if sub == 0 else 0
            outs = yield from _swa_block(
                aq_ref[rows, :], jnp.concatenate([k_prev, k_cur], axis=0),
                jnp.concatenate([v_prev, v_cur], axis=0), bias_ref, table, sink_ref)
            k_prev, v_prev = k_cur, v_cur
            outs_all.append(outs)
        return outs_all

    (ys_all, new_states), outs_all = _interleave(
        [recurrence(), attention()], [HGRN_PIECES, SWA_PIECES])

    for hd in range(HG_HEADS):
        st_ref[hd] = new_states[hd]
    for sub in range(MIX_SUB):
        rows = slice(sub * CHUNK, (sub + 1) * CHUNK)
        for hd in range(HG_HEADS):
            ya_ref[rows, hd * HG_DV:(hd + 1) * HG_DV] = ys_all[sub][hd].astype(BF16)
        for g in range(N_KV_HEADS):
            yb_ref[rows, g * GROUP_W:(g + 1) * GROUP_W] = outs_all[sub][g].astype(BF16)

    @pl.when(pos == nb - 1)
    def _():
        for hd in range(HG_HEADS):
            sfin_ref[0, hd] = st_ref[hd].T
        last = slice((MIX_SUB - 1) * CHUNK, MIX_SUB * CHUNK)
        klast_ref[0] = kc_ref[last, :].T
        vlast_ref[0] = vc_ref[last, :].T


def _mixers(sinks, hq, f, hv, og, aq, ak, av, nw, batch, seq):
    n = batch * seq
    rows = MIX_SUB * CHUNK
    nb = seq // rows
    cur = lambda w: pl.BlockSpec((rows, w), lambda s: (s, 0))
    prev = lambda w: pl.BlockSpec((CHUNK, w), lambda s: (jnp.maximum(MIX_SUB * s - 1, 0), 0))
    per_seq = lambda *shape: pl.BlockSpec((1,) + shape, lambda s: (s // nb,) + (0,) * len(shape))
    state = _nbytes((HG_HEADS, HG_DV, HG_DK), F32)
    bias = _nbytes((2, N_Q_HEADS, WINDOW, WINDOW), F32)
    vmem = _vmem_limit(
        resident=0,
        streamed=(_nbytes((rows, HG_W), F32) + 6 * _nbytes((rows, HG_W), BF16)
                  + 2 * _nbytes((rows + CHUNK, KV_W), F32) + state + 2 * _nbytes((KV_W, WINDOW), F32)),
        scratch=state + bias,
        temporaries=MIX_SUB * 10 * _nbytes((CHUNK, HG_W), F32))
    return pl.pallas_call(
        functools.partial(_mixers_kernel, nb=nb),
        grid=(batch * nb,),
        in_specs=[pl.BlockSpec(memory_space=pltpu.SMEM),
                  cur(HG_W), cur(HG_W), cur(HG_W), cur(HG_W),
                  cur(Q_W), prev(KV_W), cur(KV_W), prev(KV_W), cur(KV_W),
                  _const_spec((1, HG_W))],
        out_specs=[cur(HG_W), cur(Q_W), per_seq(HG_HEADS, HG_DK, HG_DV),
                   per_seq(KV_W, WINDOW), per_seq(KV_W, WINDOW)],
        out_shape=(jax.ShapeDtypeStruct((n, HG_W), BF16),
                   jax.ShapeDtypeStruct((n, Q_W), BF16),
                   jax.ShapeDtypeStruct((batch, HG_HEADS, HG_DK, HG_DV), F32),
                   jax.ShapeDtypeStruct((batch, KV_W, WINDOW), F32),
                   jax.ShapeDtypeStruct((batch, KV_W, WINDOW), F32)),
        scratch_shapes=[pltpu.VMEM((HG_HEADS, HG_DV, HG_DK), F32),
                        pltpu.VMEM((2, N_Q_HEADS, WINDOW, WINDOW), F32)],
        compiler_params=pltpu.CompilerParams(
            dimension_semantics=("arbitrary",), vmem_limit_bytes=vmem),
        name="mixers",
    )(sinks, hq, f, hv, og, aq, ak, ak, av, av, nw)


def _merge_ffn(x, ya, yb, ga, gb, woh, woa, wout, n2, wg, wu, wd):
    n, d = x.shape
    d_ff = wg.shape[1]
    tm = ROW_TILE
    row = lambda w: pl.BlockSpec((tm, w), lambda i: (i, 0))
    vmem = _vmem_limit(
        resident=_nbytes((HG_W + Q_W + d, d), BF16) + 3 * _nbytes((d, d_ff), BF16),
        streamed=2 * _nbytes((tm, d), F32) + _nbytes((tm, HG_W + Q_W + 2 * d), BF16),
        scratch=_nbytes((tm, d_ff), BF16),
        temporaries=5 * _nbytes((tm, d), F32))
    return pl.pallas_call(
        _merge_ffn_kernel,
        grid=(n // tm,),
        in_specs=[row(d), row(HG_W), row(Q_W), row(d), row(d),
                  _const_spec((HG_W, d)), _const_spec((Q_W, d)), _const_spec((d, d)),
                  _const_spec((1, d)), _const_spec((d, d_ff)), _const_spec((d, d_ff)),
                  _const_spec((d_ff, d))],
        out_specs=row(d),
        out_shape=jax.ShapeDtypeStruct((n, d), F32),
        scratch_shapes=[pltpu.VMEM((tm, d_ff), BF16)],
        compiler_params=pltpu.CompilerParams(
            dimension_semantics=("arbitrary",), vmem_limit_bytes=vmem),
        name="merge_ffn",
    )(x, ya, yb, ga, gb, woh, woa, wout, n2, wg, wu, wd)


def kernel(x_prompt, x_sample, state_hgrn, cache_k, cache_v, norm1_w, w_in, lb_logits, hgrn_norm_w,
           q_norm_w, k_norm_w, sinks, w_o_hgrn, w_o_attn, w_out, norm2_w, w_ffn_gate, w_ffn_up,
           w_ffn_down):
    depth = w_in.shape[0]
    assert depth == 1 and lb_logits.shape[0] == 2, "single-layer step only"
    batch, seq, d = x_prompt.shape
    nseq, dec_seq, _ = x_sample.shape
    cw = cache_k.shape[2]
    assert seq % ROW_TILE == 0 and seq % CHUNK == 0 and seq >= WINDOW
    assert dec_seq <= SEQ_PAD and cw == WINDOW and w_ffn_gate.shape[2] % FFN_CHUNK == 0
    assert (nseq * SEQ_PAD) % ROW_TILE == 0 and nseq % SWA_SAMPLE_SEQS == 0

    win = w_in[0].astype(BF16)
    n1 = norm1_w[0][None, :]
    n2 = norm2_w[0][None, :]
    qn = jnp.tile(q_norm_w[0], N_Q_HEADS)[None, :] * (HEAD_DIM ** -0.5 * LOG2E)
    kn = jnp.tile(k_norm_w[0], N_KV_HEADS)[None, :]
    nw = hgrn_norm_w[0][None, :]
    head_of = jnp.arange(Q_W, dtype=jnp.int32) // HEAD_DIM
    hsel = jnp.where(head_of[:, None] == head_of[None, :], 1.0 / HEAD_DIM, 0.0).astype(BF16)
    sk = sinks[0].astype(F32)
    woh, woa, wout = w_o_hgrn[0].astype(BF16), w_o_attn[0].astype(BF16), w_out[0].astype(BF16)
    wg, wu, wd = w_ffn_gate[0].astype(BF16), w_ffn_up[0].astype(BF16), w_ffn_down[0].astype(BF16)

    xp = x_prompt.reshape(batch * seq, d)
    hq, f, hv, og, aq, ak, av, ga, gb = _proj(xp, n1, win, lb_logits, qn, kn, hsel)
    ya, yb, s_prompt, kt_prompt, vt_prompt = _mixers(sk, hq, f, hv, og, aq, ak, av, nw, batch, seq)
    y_prompt = _merge_ffn(xp, ya, yb, ga, gb, woh, woa, wout, n2, wg, wu, wd).reshape(batch, seq, d)

    def from_transposed(t):
        return t.reshape(t.shape[0], N_KV_HEADS, HEAD_DIM, t.shape[2]).transpose(0, 3, 1, 2)

    def to_transposed(c):
        return c.transpose(0, 2, 3, 1).reshape(c.shape[0], KV_W, c.shape[1])

    xs = jnp.pad(x_sample, ((0, 0), (0, SEQ_PAD - dec_seq), (0, 0))).reshape(nseq * SEQ_PAD, d)
    hq, f, hv, og, aq, ak, av, ga, gb = _proj(xs, n1, win, lb_logits, qn, kn, hsel)
    ya, s_sample = _hgrn_sample(hq, f, hv, og, nw, state_hgrn[0], dec_seq)
    yb, kt_sample, vt_sample = _swa_sample(
        sk, aq.reshape(nseq, SEQ_PAD, Q_W), ak.reshape(nseq, SEQ_PAD, KV_W),
        av.reshape(nseq, SEQ_PAD, KV_W), to_transposed(cache_k[0]), to_transposed(cache_v[0]),
        dec_seq)
    ys = _merge_ffn(xs, ya, yb.reshape(nseq * SEQ_PAD, Q_W), ga, gb, woh, woa, wout, n2, wg, wu, wd)
    y_sample = ys.reshape(nseq, SEQ_PAD, d)[:, :dec_seq]

    return (y_prompt, y_sample, s_prompt[None], from_transposed(kt_prompt)[None],
            from_transposed(vt_prompt)[None], s_sample[None], from_transposed(kt_sample)[None],
            from_transposed(vt_sample)[None])
```

```python
import functools
import math

import jax
import jax.numpy as jnp
from jax import lax
from jax.experimental import pallas as pl
from jax.experimental.pallas import tpu as pltpu

F32 = jnp.float32
BF16 = jnp.bfloat16

EPS = 1e-6
NEG_BIG = -1e30

HG_HEADS = 4
HG_DK = 128
HG_DV = 128
HG_W = HG_HEADS * HG_DK
N_Q_HEADS = 8
N_KV_HEADS = 2
Q_PER_KV = N_Q_HEADS // N_KV_HEADS
HEAD_DIM = 64
HEAD_SHIFT = HEAD_DIM.bit_length() - 1
WINDOW = 128
Q_W = N_Q_HEADS * HEAD_DIM
KV_W = N_KV_HEADS * HEAD_DIM
GROUP_W = Q_PER_KV * HEAD_DIM

SEQ_PAD = 8
ROW_TILE = 512
CHUNK = 128
FFN_CHUNK = 256
DOWN_TILE = 1024
MIX_SUB = 8
PROJ_TILE = 256
HEAD_GROUP = 2
HGRN_PIECES = MIX_SUB * 20
SWA_PIECES = MIX_SUB * 18
DIAG = 64
V7X_VMEM_BYTES = 64 * 1024 * 1024
SCOPED_VMEM_BYTES = V7X_VMEM_BYTES - 4 * 1024 * 1024


def _nbytes(shape, dtype):
    return math.prod(shape) * jnp.dtype(dtype).itemsize


def _vmem_limit(resident, streamed, scratch, temporaries):
    need = resident + 2 * streamed + scratch + temporaries
    assert need <= SCOPED_VMEM_BYTES, need
    return SCOPED_VMEM_BYTES


def _nt_dot(a, b):
    return lax.dot_general(a, b, (((1,), (1,)), ((), ())), preferred_element_type=F32)


def _tn_dot(a, b):
    return lax.dot_general(a, b, (((0,), (0,)), ((), ())), preferred_element_type=F32)


def _sigmoid(x):
    return 1.0 / (1.0 + jnp.exp(-x))


def _drain(gen):
    try:
        while True:
            next(gen)
    except StopIteration as stop:
        return stop.value


def _interleave(gens, pieces):
    done = [0] * len(gens)
    results = [None] * len(gens)
    live = list(range(len(gens)))
    while live:
        i = min(live, key=lambda g: ((done[g] + 1) / pieces[g], g))
        try:
            next(gens[i])
            done[i] += 1
        except StopIteration as stop:
            results[i] = stop.value
            live.remove(i)
    return results


def _const_spec(shape):
    nd = len(shape)
    return pl.BlockSpec(shape, lambda *_: (0,) * nd, pipeline_mode=pl.Buffered(1))


def _proj_rows(rows, x_ref, n1_ref, win_ref, lbl_ref, qn_ref, kn_ref, hsel_ref,
               hq_ref, f_ref, hv_ref, og_ref, aq_ref, ak_ref, av_ref, ga_ref, gb_ref):
    x = x_ref[rows, :]
    h = x * lax.rsqrt(jnp.mean(x * x, axis=-1, keepdims=True) + EPS) * n1_ref[...]
    hb = h.astype(BF16)
    yield

    w = PROJ_TILE
    col = [0]

    def proj():
        res = jnp.dot(hb, win_ref[:, col[0]:col[0] + w], preferred_element_type=F32)
        col[0] += w
        return res

    lg = lbl_ref[...]
    e = jnp.exp(lg - jnp.max(lg, axis=0, keepdims=True))
    lb = e[0:1, :] / jnp.sum(e, axis=0, keepdims=True)

    for c in range(0, HG_W, w):
        qa = proj()
        yield
        hq_ref[rows, c:c + w] = (qa * _sigmoid(qa) * (HG_DK ** -0.5)).astype(BF16)
        yield
    for c in range(0, HG_W, w):
        fa = proj()
        yield
        f_ref[rows, c:c + w] = lb[:, c:c + w] + (1.0 - lb[:, c:c + w]) * _sigmoid(fa)
        yield
    for c in range(0, HG_W, w):
        hv_ref[rows, c:c + w] = proj().astype(BF16)
        yield
    for c in range(0, HG_W, w):
        go = proj()
        yield
        og_ref[rows, c:c + w] = (go * _sigmoid(go)).astype(BF16)
        yield

    for c in range(0, Q_W, w):
        qb = proj()
        yield
        msq = jnp.dot((qb * qb).astype(BF16), hsel_ref[0:w, 0:w], preferred_element_type=F32)
        aq_ref[rows, c:c + w] = (qb * lax.rsqrt(msq + EPS) * qn_ref[:, c:c + w]).astype(BF16)
        yield
    assert 2 * KV_W == w
    kv = proj()
    yield
    kb = kv[:, 0:KV_W]
    msk = jnp.dot((kb * kb).astype(BF16), hsel_ref[0:KV_W, 0:KV_W], preferred_element_type=F32)
    ak_ref[rows, :] = kb * lax.rsqrt(msk + EPS) * kn_ref[...]
    av_ref[rows, :] = kv[:, KV_W:2 * KV_W]
    yield

    d = x.shape[-1]
    for gate_ref in (ga_ref, gb_ref):
        for c in range(0, d, w):
            g = proj()
            yield
            gate_ref[rows, c:c + w] = _sigmoid(g).astype(BF16)
            yield


def _proj_kernel(*refs):
    half = ROW_TILE // 2
    first = _proj_rows(slice(0, half), *refs)
    second = _proj_rows(slice(half, ROW_TILE), *refs)
    next(first)
    _interleave([first, second], [1, 1])


def _proj(x, n1, win, lbl, qn, kn, hsel):
    n, d = x.shape
    pw = win.shape[1]
    tm = ROW_TILE
    row = lambda w: pl.BlockSpec((tm, w), lambda i: (i, 0))
    out_shapes = (
        jax.ShapeDtypeStruct((n, HG_W), BF16),
        jax.ShapeDtypeStruct((n, HG_W), F32),
        jax.ShapeDtypeStruct((n, HG_W), BF16),
        jax.ShapeDtypeStruct((n, HG_W), BF16),
        jax.ShapeDtypeStruct((n, Q_W), BF16),
        jax.ShapeDtypeStruct((n, KV_W), F32),
        jax.ShapeDtypeStruct((n, KV_W), F32),
        jax.ShapeDtypeStruct((n, d), BF16),
        jax.ShapeDtypeStruct((n, d), BF16),
    )
    vmem = _vmem_limit(
        resident=_nbytes((d, pw), BF16) + _nbytes((Q_W, Q_W), BF16),
        streamed=_nbytes((tm, d), F32) + sum(_nbytes((tm, s.shape[1]), s.dtype) for s in out_shapes),
        scratch=0,
        temporaries=3 * _nbytes((tm, d), F32))
    return pl.pallas_call(
        _proj_kernel,
        grid=(n // tm,),
        in_specs=[row(d), _const_spec((1, d)), _const_spec((d, pw)), _const_spec(lbl.shape),
                  _const_spec((1, Q_W)), _const_spec((1, KV_W)), _const_spec((Q_W, Q_W))],
        out_specs=[row(HG_W), row(HG_W), row(HG_W), row(HG_W), row(Q_W), row(KV_W), row(KV_W),
                   row(d), row(d)],
        out_shape=out_shapes,
        compiler_params=pltpu.CompilerParams(
            dimension_semantics=("arbitrary",), vmem_limit_bytes=vmem),
        name="proj",
    )(x, n1, win, lbl, qn, kn, hsel)


def _roll_rows(x, d):
    d = d % x.shape[0]
    return x if d == 0 else pltpu.roll(x, d, axis=0)


SUBLANES = 8


def _slabs(x):
    return [x[r * SUBLANES:(r + 1) * SUBLANES, :] for r in range(x.shape[0] // SUBLANES)]


def _pair_levels(rows):
    ti = lax.broadcasted_iota(jnp.int32, (rows, rows), 0)
    tj = lax.broadcasted_iota(jnp.int32, (rows, rows), 1)
    lv = 31 - lax.clz(ti ^ tj)
    return jnp.where(ti > tj, lv, jnp.where(ti == tj, DIAG, -1))


def _hgrn_levels(qh, fh, lvl_slabs, nlev):
    ns = qh.shape[0] // SUBLANES
    kh = 1.0 - fh
    qs, ks = _slabs(qh), _slabs(kh)
    incl, tot, rexc = _slabs(fh), _slabs(fh), [None] * ns
    t8 = lax.broadcasted_iota(jnp.int32, (SUBLANES, qh.shape[1]), 0)
    zero = jnp.zeros((SUBLANES, qh.shape[1]), F32)
    qb, kb = qh.astype(BF16), kh.astype(BF16)
    diag = _slabs(_nt_dot(qb, kb))
    p = [jnp.where(lvl_slabs[r] == DIAG, diag[r], 0.0) for r in range(ns)]
    for m in range(nlev):
        h = 1 << m
        hs = h // SUBLANES
        upper = [hs == 0 or (r // hs) % 2 == 1 for r in range(ns)]
        lower = [hs == 0 or (r // hs) % 2 == 0 for r in range(ns)]
        a = jnp.concatenate([qs[r] * incl[r] if upper[r] else zero for r in range(ns)], axis=0)
        if m == 0:
            b = kb
        else:
            b = jnp.concatenate([ks[r] * rexc[r] if lower[r] else zero for r in range(ns)],
                                axis=0).astype(BF16)
        sc = _slabs(_nt_dot(a.astype(BF16), b))
        p = [jnp.where(lvl_slabs[r] == m, sc[r], p[r]) if upper[r] else p[r] for r in range(ns)]
        yield
        if hs == 0:
            up = (t8 & h) != 0
            for r in range(ns):
                below = pltpu.roll(tot[r], h, axis=0)
                above = pltpu.roll(tot[r], SUBLANES - h, axis=0)
                incl[r] = incl[r] * jnp.where(up, below, 1.0)
                grow = jnp.where(up, 1.0, above)
                rexc[r] = grow if rexc[r] is None else rexc[r] * grow
                tot[r] = tot[r] * jnp.where(up, below, above)
        else:
            for b0 in range(0, ns, 2 * hs):
                t_lo, t_hi = tot[b0], tot[b0 + hs]
                both = t_lo * t_hi
                for r in range(b0, b0 + hs):
                    rexc[r] = rexc[r] * t_hi
                    tot[r] = both
                for r in range(b0 + hs, b0 + 2 * hs):
                    incl[r] = incl[r] * t_lo
                    tot[r] = both
    return jnp.concatenate(p, axis=0), qs, ks, incl, rexc, tot


def _hgrn_finish(o, nw, og):
    return o * lax.rsqrt(jnp.mean(o * o, axis=-1, keepdims=True) + EPS) * nw * og


def _hgrn_chunk(q, f, v, og, nw, states):
    nlev = CHUNK.bit_length() - 1
    lvl_slabs = _slabs(_pair_levels(CHUNK))
    ys, new_states = [], []
    for h0 in range(0, HG_HEADS, HEAD_GROUP):
        group = range(h0, h0 + HEAD_GROUP)
        sweeps = {hd: _hgrn_levels(q[:, hd * HG_DK:(hd + 1) * HG_DK],
                                   f[:, hd * HG_DK:(hd + 1) * HG_DK], lvl_slabs, nlev)
                  for hd in group}
        swept = {}
        for _ in range(nlev + 1):
            for hd in group:
                try:
                    next(sweeps[hd])
                except StopIteration as stop:
                    swept[hd] = stop.value
            yield
        for hd in group:
            sl = slice(hd * HG_DK, (hd + 1) * HG_DK)
            p, qs, ks, incl, rexc, tot = swept[hd]
            ns = len(qs)
            qd = jnp.concatenate([qs[r] * incl[r] for r in range(ns)], axis=0).astype(BF16)
            kd = jnp.concatenate([ks[r] * rexc[r] for r in range(ns)], axis=0).astype(BF16)
            st = states[hd]
            o = (jnp.dot(p.astype(BF16), v[:, sl], preferred_element_type=F32)
                 + _nt_dot(qd, st.astype(BF16)))
            new_states.append(st * tot[0][0:1, :] + _tn_dot(v[:, sl], kd))
            ys.append(_hgrn_finish(o, nw[:, sl], og[:, sl]))
            yield
    return ys, new_states


def _hgrn_sample_kernel(hq_ref, f_ref, hv_ref, og_ref, nw_ref, s_ref, ya_ref, so_ref, *, seq_len):
    nlev = SEQ_PAD.bit_length() - 1
    t = lax.broadcasted_iota(jnp.int32, (CHUNK, HG_W), 0)
    f = jnp.where((t & (SEQ_PAD - 1)) < seq_len, f_ref[...], 1.0)
    q = hq_ref[...].astype(F32)
    v = hv_ref[...]
    vf = v.astype(F32)
    og = og_ref[...].astype(F32)
    nw = nw_ref[...]
    lvl_slabs = _slabs(_pair_levels(CHUNK))
    assert SEQ_PAD == SUBLANES
    for hd in range(HG_HEADS):
        sl = slice(hd * HG_DK, (hd + 1) * HG_DK)
        p, qs, ks, incl, rexc, tot = _drain(_hgrn_levels(q[:, sl], f[:, sl], lvl_slabs, nlev))
        o_inter = []
        for s in range(CHUNK // SEQ_PAD):
            rows = slice(s * SEQ_PAD, (s + 1) * SEQ_PAD)
            st = s_ref[s, hd]
            o_inter.append(jnp.dot(qs[s] * incl[s], st, preferred_element_type=F32))
            decay = jnp.broadcast_to(tot[s][0:1, :], (HG_DV, HG_DK)).T
            so_ref[s, hd] = st * decay + _tn_dot(ks[s] * rexc[s], vf[rows, sl])
        o = (jnp.dot(p.astype(BF16), v[:, sl], preferred_element_type=F32)
             + jnp.concatenate(o_inter, axis=0))
        ya_ref[:, sl] = _hgrn_finish(o, nw[:, sl], og[:, sl]).astype(BF16)


def _hgrn_sample(hq, f, hv, og, nw, state, seq_len):
    n = hq.shape[0]
    spc = CHUNK // SEQ_PAD
    blk = pl.BlockSpec((CHUNK, HG_W), lambda i: (i, 0))
    sblk = pl.BlockSpec((spc, HG_HEADS, HG_DK, HG_DV), lambda i: (i, 0, 0, 0))
    state_block = _nbytes((spc, HG_HEADS, HG_DK, HG_DV), F32)
    vmem = _vmem_limit(
        resident=0,
        streamed=2 * state_block + _nbytes((CHUNK, HG_W), F32) + 4 * _nbytes((CHUNK, HG_W), BF16),
        scratch=0,
        temporaries=16 * _nbytes((CHUNK, HG_W), F32))
    return pl.pallas_call(
        functools.partial(_hgrn_sample_kernel, seq_len=seq_len),
        grid=(n // CHUNK,),
        in_specs=[blk, blk, blk, blk, _const_spec((1, HG_W)), sblk],
        out_specs=[blk, sblk],
        out_shape=(jax.ShapeDtypeStruct((n, HG_W), BF16),
                   jax.ShapeDtypeStruct(state.shape, F32)),
        compiler_params=pltpu.CompilerParams(
            dimension_semantics=("arbitrary",), vmem_limit_bytes=vmem),
        name="hgrn_sample",
    )(hq, f, hv, og, nw, state)


LOG2E = 1.4426950408889634


def _alibi_slope(head):
    return LOG2E * 2.0 ** (-8.0 * (head + 1) / N_Q_HEADS)


def _tile_kv_head(x, g):
    lane = lax.broadcasted_iota(jnp.int32, x.shape, 1)
    other = pltpu.roll(x, HEAD_DIM, axis=1)
    mine = (lane >> HEAD_SHIFT) == g
    one = jnp.where(mine, x, other).astype(BF16)
    return jnp.concatenate([one] * (GROUP_W // KV_W), axis=1)


def _attn_group(qg, kt, vt, bias_of, sink_of):
    lane_head = lax.broadcasted_iota(jnp.int32, kt.shape, 1) >> HEAD_SHIFT
    zero = jnp.zeros_like(kt)
    vbd = jnp.concatenate([jnp.where(lane_head == hh, vt, zero) for hh in range(Q_PER_KV)], axis=0)
    tq = lax.broadcasted_iota(jnp.int32, (WINDOW, WINDOW), 0)
    tk = lax.broadcasted_iota(jnp.int32, (WINDOW, WINDOW), 1)
    in_cur = tk <= tq
    ps, invs = [], []
    for hh in range(Q_PER_KV):
        s2 = _nt_dot(qg, jnp.where(lane_head == hh, kt, zero))
        s = jnp.where(in_cur, s2[:, WINDOW:], s2[:, :WINDOW]) + bias_of(hh)
        sink = sink_of(hh) * LOG2E
        m = jnp.maximum(jnp.max(s, axis=-1, keepdims=True), sink)
        yield
        p = jnp.exp2(s - m)
        denom = jnp.sum(p, axis=-1, keepdims=True) + jnp.exp2(sink - m)
        ps.append(jnp.where(in_cur, 0.0, p).astype(BF16))
        ps.append(jnp.where(in_cur, p, 0.0).astype(BF16))
        invs.append(1.0 / denom)
        yield
    o = jnp.dot(jnp.concatenate(ps, axis=1), vbd, preferred_element_type=F32)
    out_head = lax.broadcasted_iota(jnp.int32, o.shape, 1) >> HEAD_SHIFT
    scale = invs[Q_PER_KV - 1]
    for hh in range(Q_PER_KV - 2, -1, -1):
        scale = jnp.where(out_head == hh, invs[hh], scale)
    yield
    return o * scale


def _init_band_bias(bias_ref):
    tq = lax.broadcasted_iota(jnp.int32, (WINDOW, WINDOW), 0)
    c = lax.broadcasted_iota(jnp.int32, (WINDOW, WINDOW), 1)
    in_cur = c <= tq
    dist = jnp.where(in_cur, tq - c, WINDOW + tq - c)
    for h in range(N_Q_HEADS):
        bias = -_alibi_slope(h) * dist.astype(F32)
        bias_ref[0, h] = bias
        bias_ref[1, h] = jnp.where(in_cur, bias, NEG_BIG)


def _swa_block(q, k2, v2, bias_ref, table, sink_ref):
    outs = []
    for g in range(N_KV_HEADS):
        kt = _tile_kv_head(k2, g)
        vt = _tile_kv_head(v2, g)
        o = yield from _attn_group(q[:, g * GROUP_W:(g + 1) * GROUP_W], kt, vt,
                                   lambda hh, g=g: bias_ref[table, g * Q_PER_KV + hh],
                                   lambda hh, g=g: sink_ref[g * Q_PER_KV + hh])
        outs.append(o)
    return outs


SWA_SAMPLE_SEQS = 16
SWA_SAMPLE_BATCH = 8
STACK = Q_PER_KV * SEQ_PAD


def _swa_sample_kernel(sink_ref, q_ref, kn_ref, vn_ref, ckt_ref, cvt_ref,
                       y_ref, ckot_ref, cvot_ref, bias_ref, sinkv_ref, *, seq_len):
    cw = ckt_ref.shape[2]

    @pl.when(pl.program_id(0) == 0)
    def _():
        rows = SWA_SAMPLE_BATCH * STACK
        r = lax.broadcasted_iota(jnp.int32, (rows, cw), 0)
        c = lax.broadcasted_iota(jnp.int32, (rows, cw), 1)
        tq = r & (SEQ_PAD - 1)
        hh = (r >> (SEQ_PAD.bit_length() - 1)) & (Q_PER_KV - 1)
        dist_c = cw + tq - c
        valid_c = (dist_c >= 0) & (dist_c < WINDOW)
        dist_n = tq - c
        valid_n = (dist_n >= 0) & (dist_n < WINDOW) & (c < seq_len)
        for g in range(N_KV_HEADS):
            slope = jnp.full((rows, cw), _alibi_slope(g * Q_PER_KV + Q_PER_KV - 1), F32)
            sink = jnp.full((rows, cw), sink_ref[g * Q_PER_KV + Q_PER_KV - 1] * LOG2E, F32)
            for h in range(Q_PER_KV - 2, -1, -1):
                slope = jnp.where(hh == h, _alibi_slope(g * Q_PER_KV + h), slope)
                sink = jnp.where(hh == h, sink_ref[g * Q_PER_KV + h] * LOG2E, sink)
            bias_ref[g, 0] = jnp.where(valid_c, -slope * dist_c.astype(F32), NEG_BIG)
            bias_ref[g, 1] = jnp.where(valid_n, -slope * dist_n.astype(F32), NEG_BIG)
            sinkv_ref[g] = sink

    lane = lax.broadcasted_iota(jnp.int32, (KV_W, cw), 1)
    keep = lane < cw - seq_len
    pad = jnp.zeros((cw - SEQ_PAD, KV_W), F32)

    nb = SWA_SAMPLE_BATCH
    shift = cw - seq_len

    def body(i, carry):
        seqs = [i * nb + j for j in range(nb)]
        q = [q_ref[s].astype(F32) for s in seqs]
        k_new = [jnp.concatenate([kn_ref[s], pad], axis=0) for s in seqs]
        v_new = [jnp.concatenate([vn_ref[s], pad], axis=0) for s in seqs]
        ckt = [ckt_ref[s] for s in seqs]
        cvt = [cvt_ref[s] for s in seqs]
        for g in range(N_KV_HEADS):
            ch = slice(g * HEAD_DIM, (g + 1) * HEAD_DIM)
            qs = [jnp.concatenate(
                [qj[:, (g * Q_PER_KV + h) * HEAD_DIM:(g * Q_PER_KV + h + 1) * HEAD_DIM]
                 for h in range(Q_PER_KV)], axis=0).astype(BF16) for qj in q]
            s_c = jnp.concatenate(
                [jnp.dot(qs[j], ckt[j][ch, :].astype(BF16), preferred_element_type=F32)
                 for j in range(nb)], axis=0) + bias_ref[g, 0]
            s_n = jnp.concatenate(
                [_nt_dot(qs[j], k_new[j][:, ch].astype(BF16)) for j in range(nb)],
                axis=0) + bias_ref[g, 1]
            sink = sinkv_ref[g][:, 0:1]
            m = jnp.maximum(jnp.maximum(jnp.max(s_c, axis=-1, keepdims=True),
                                        jnp.max(s_n, axis=-1, keepdims=True)), sink)
            p_c = jnp.exp2(s_c - m)
            p_n = jnp.exp2(s_n - m)
            denom = (jnp.sum(p_c, axis=-1, keepdims=True) + jnp.sum(p_n, axis=-1, keepdims=True)
                     + jnp.exp2(sink - m))
            inv = 1.0 / denom
            p_c, p_n = p_c.astype(BF16), p_n.astype(BF16)
            for j in range(nb):
                rows = slice(j * STACK, (j + 1) * STACK)
                o = (_nt_dot(p_c[rows, :], cvt[j][ch, :].astype(BF16))
                     + jnp.dot(p_n[rows, :], v_new[j][:, ch].astype(BF16),
                               preferred_element_type=F32)) * inv[rows, :]
                y = jnp.concatenate(
                    [o[h * SEQ_PAD:(h + 1) * SEQ_PAD, :] for h in range(Q_PER_KV)], axis=1)
                y_ref[seqs[j], :, g * GROUP_W:(g + 1) * GROUP_W] = y.astype(BF16)
        for j in range(nb):
            ckot_ref[seqs[j]] = jnp.where(keep, pltpu.roll(ckt[j], shift, axis=1),
                                          _roll_rows(k_new[j], shift).T)
            cvot_ref[seqs[j]] = jnp.where(keep, pltpu.roll(cvt[j], shift, axis=1),
                                          _roll_rows(v_new[j], shift).T)
        return carry

    lax.fori_loop(0, q_ref.shape[0] // nb, body, 0)


def _swa_sample(sinks, aq, ak, av, cache_kt, cache_vt, seq_len):
    nseq, _, cw = cache_kt.shape
    g = SWA_SAMPLE_SEQS
    blk = lambda r, w: pl.BlockSpec((g, r, w), lambda i: (i, 0, 0))
    rows = SWA_SAMPLE_BATCH * STACK
    table = _nbytes((N_KV_HEADS, rows, cw), F32)
    vmem = _vmem_limit(
        resident=0,
        streamed=(4 * _nbytes((g, KV_W, cw), F32) + 2 * _nbytes((g, SEQ_PAD, Q_W), BF16)
                  + 2 * _nbytes((g, SEQ_PAD, KV_W), F32)),
        scratch=3 * table,
        temporaries=(4 * SWA_SAMPLE_BATCH + 8) * _nbytes((KV_W, cw), F32))
    return pl.pallas_call(
        functools.partial(_swa_sample_kernel, seq_len=seq_len),
        grid=(nseq // g,),
        in_specs=[pl.BlockSpec(memory_space=pltpu.SMEM),
                  blk(SEQ_PAD, Q_W), blk(SEQ_PAD, KV_W), blk(SEQ_PAD, KV_W),
                  blk(KV_W, cw), blk(KV_W, cw)],
        out_specs=[blk(SEQ_PAD, Q_W), blk(KV_W, cw), blk(KV_W, cw)],
        out_shape=(jax.ShapeDtypeStruct((nseq, SEQ_PAD, Q_W), BF16),
                   jax.ShapeDtypeStruct((nseq, KV_W, cw), F32),
                   jax.ShapeDtypeStruct((nseq, KV_W, cw), F32)),
        scratch_shapes=[pltpu.VMEM((N_KV_HEADS, 2, rows, cw), F32),
                        pltpu.VMEM((N_KV_HEADS, rows, cw), F32)],
        compiler_params=pltpu.CompilerParams(
            dimension_semantics=("arbitrary",), vmem_limit_bytes=vmem),
        name="swa_sample",
    )(sinks, aq, ak, av, cache_kt, cache_vt)


def _merge_ffn_rows(x, ya, yb, ga, gb, woh_ref, woa_ref, wout_ref, n2_ref, wg_ref, wu_ref, wd_ref,
                    act_ref):
    branch_a = jnp.dot(ya, woh_ref[...], preferred_element_type=F32)
    yield
    branch_b = jnp.dot(yb, woa_ref[...], preferred_element_type=F32)
    yield
    merged = ga.astype(F32) * branch_a + gb.astype(F32) * branch_b
    x1 = x + jnp.dot(merged.astype(BF16), wout_ref[...], preferred_element_type=F32)
    yield
    h2 = (x1 * lax.rsqrt(jnp.mean(x1 * x1, axis=-1, keepdims=True) + EPS) * n2_ref[...]).astype(BF16)
    d_ff = wg_ref.shape[1]
    for c in range(d_ff // FFN_CHUNK):
        cols = slice(c * FFN_CHUNK, (c + 1) * FFN_CHUNK)
        gate = jnp.dot(h2, wg_ref[:, cols], preferred_element_type=F32)
        yield
        up = jnp.dot(h2, wu_ref[:, cols], preferred_element_type=F32)
        yield
        act_ref[:, cols] = (gate * _sigmoid(gate) * up).astype(BF16)
        yield
    act = act_ref[...]
    outs = []
    for n in range(x.shape[1] // DOWN_TILE):
        cols = slice(n * DOWN_TILE, (n + 1) * DOWN_TILE)
        outs.append(x1[:, cols] + jnp.dot(act, wd_ref[:, cols], preferred_element_type=F32))
        yield
    return jnp.concatenate(outs, axis=1)


def _merge_ffn_kernel(x_ref, ya_ref, yb_ref, ga_ref, gb_ref, woh_ref, woa_ref, wout_ref,
                      n2_ref, wg_ref, wu_ref, wd_ref, o_ref, act_s):
    o_ref[...] = _drain(_merge_ffn_rows(
        x_ref[...], ya_ref[...], yb_ref[...], ga_ref[...], gb_ref[...],
        woh_ref, woa_ref, wout_ref, n2_ref, wg_ref, wu_ref, wd_ref, act_s))


def _mixers_kernel(sink_ref, hq_ref, f_ref, hv_ref, og_ref, aq_ref, kp_ref, kc_ref, vp_ref, vc_ref,
                   nw_ref, ya_ref, yb_ref, sfin_ref, klast_ref, vlast_ref, st_ref, bias_ref, *, nb):
    s = pl.program_id(0)

    @pl.when(s == 0)
    def _():
        _init_band_bias(bias_ref)
        st_ref[...] = jnp.zeros_like(st_ref)

    pos = lax.rem(s, nb)
    first = pos == 0

    def recurrence():
        states = [jnp.where(first, 0.0, st_ref[hd]) for hd in range(HG_HEADS)]
        ys_all = []
        for sub in range(MIX_SUB):
            rows = slice(sub * CHUNK, (sub + 1) * CHUNK)
            ys, states = yield from _hgrn_chunk(
                hq_ref[rows, :].astype(F32), f_ref[rows, :], hv_ref[rows, :],
                og_ref[rows, :].astype(F32), nw_ref[...], states)
            ys_all.append(ys)
        return ys_all, states

    def attention():
        k_prev, v_prev = kp_ref[...], vp_ref[...]
        outs_all = []
        for sub in range(MIX_SUB):
            rows = slice(sub * CHUNK, (sub + 1) * CHUNK)
            k_cur, v_cur = kc_ref[rows, :], vc_ref[rows, :]
            table = jnp.where(first, 1, 0) if sub == 0 else 0
            outs = yield from _swa_block(
                aq_ref[rows, :], jnp.concatenate([k_prev, k_cur], axis=0),
                jnp.concatenate([v_prev, v_cur], axis=0), bias_ref, table, sink_ref)
            k_prev, v_prev = k_cur, v_cur
            outs_all.append(outs)
        return outs_all

    (ys_all, new_states), outs_all = _interleave(
        [recurrence(), attention()], [HGRN_PIECES, SWA_PIECES])

    for hd in range(HG_HEADS):
        st_ref[hd] = new_states[hd]
    for sub in range(MIX_SUB):
        rows = slice(sub * CHUNK, (sub + 1) * CHUNK)
        for hd in range(HG_HEADS):
            ya_ref[rows, hd * HG_DV:(hd + 1) * HG_DV] = ys_all[sub][hd].astype(BF16)
        for g in range(N_KV_HEADS):
            yb_ref[rows, g * GROUP_W:(g + 1) * GROUP_W] = outs_all[sub][g].astype(BF16)

    @pl.when(pos == nb - 1)
    def _():
        for hd in range(HG_HEADS):
            sfin_ref[0, hd] = st_ref[hd].T
        last = slice((MIX_SUB - 1) * CHUNK, MIX_SUB * CHUNK)
        klast_ref[0] = kc_ref[last, :].T
        vlast_ref[0] = vc_ref[last, :].T


def _mixers(sinks, hq, f, hv, og, aq, ak, av, nw, batch, seq):
    n = batch * seq
    rows = MIX_SUB * CHUNK
    nb = seq // rows
    cur = lambda w: pl.BlockSpec((rows, w), lambda s: (s, 0))
    prev = lambda w: pl.BlockSpec((CHUNK, w), lambda s: (jnp.maximum(MIX_SUB * s - 1, 0), 0))
    per_seq = lambda *shape: pl.BlockSpec((1,) + shape, lambda s: (s // nb,) + (0,) * len(shape))
    state = _nbytes((HG_HEADS, HG_DV, HG_DK), F32)
    bias = _nbytes((2, N_Q_HEADS, WINDOW, WINDOW), F32)
    vmem = _vmem_limit(
        resident=0,
        streamed=(_nbytes((rows, HG_W), F32) + 6 * _nbytes((rows, HG_W), BF16)
                  + 2 * _nbytes((rows + CHUNK, KV_W), F32) + state + 2 * _nbytes((KV_W, WINDOW), F32)),
        scratch=state + bias,
        temporaries=MIX_SUB * 10 * _nbytes((CHUNK, HG_W), F32))
    return pl.pallas_call(
        functools.partial(_mixers_kernel, nb=nb),
        grid=(batch * nb,),
        in_specs=[pl.BlockSpec(memory_space=pltpu.SMEM),
                  cur(HG_W), cur(HG_W), cur(HG_W), cur(HG_W),
                  cur(Q_W), prev(KV_W), cur(KV_W), prev(KV_W), cur(KV_W),
                  _const_spec((1, HG_W))],
        out_specs=[cur(HG_W), cur(Q_W), per_seq(HG_HEADS, HG_DK, HG_DV),
                   per_seq(KV_W, WINDOW), per_seq(KV_W, WINDOW)],
        out_shape=(jax.ShapeDtypeStruct((n, HG_W), BF16),
                   jax.ShapeDtypeStruct((n, Q_W), BF16),
                   jax.ShapeDtypeStruct((batch, HG_HEADS, HG_DK, HG_DV), F32),
                   jax.ShapeDtypeStruct((batch, KV_W, WINDOW), F32),
                   jax.ShapeDtypeStruct((batch, KV_W, WINDOW), F32)),
        scratch_shapes=[pltpu.VMEM((HG_HEADS, HG_DV, HG_DK), F32),
                        pltpu.VMEM((2, N_Q_HEADS, WINDOW, WINDOW), F32)],
        compiler_params=pltpu.CompilerParams(
            dimension_semantics=("arbitrary",), vmem_limit_bytes=vmem),
        name="mixers",
    )(sinks, hq, f, hv, og, aq, ak, ak, av, av, nw)


def _merge_ffn(x, ya, yb, ga, gb, woh, woa, wout, n2, wg, wu, wd):
    n, d = x.shape
    d_ff = wg.shape[1]
    tm = ROW_TILE
    row = lambda w: pl.BlockSpec((tm, w), lambda i: (i, 0))
    vmem = _vmem_limit(
        resident=_nbytes((HG_W + Q_W + d, d), BF16) + 3 * _nbytes((d, d_ff), BF16),
        streamed=2 * _nbytes((tm, d), F32) + _nbytes((tm, HG_W + Q_W + 2 * d), BF16),
        scratch=_nbytes((tm, d_ff), BF16),
        temporaries=5 * _nbytes((tm, d), F32))
    return pl.pallas_call(
        _merge_ffn_kernel,
        grid=(n // tm,),
        in_specs=[row(d), row(HG_W), row(Q_W), row(d), row(d),
                  _const_spec((HG_W, d)), _const_spec((Q_W, d)), _const_spec((d, d)),
                  _const_spec((1, d)), _const_spec((d, d_ff)), _const_spec((d, d_ff)),
                  _const_spec((d_ff, d))],
        out_specs=row(d),
        out_shape=jax.ShapeDtypeStruct((n, d), F32),
        scratch_shapes=[pltpu.VMEM((tm, d_ff), BF16)],
        compiler_params=pltpu.CompilerParams(
            dimension_semantics=("arbitrary",), vmem_limit_bytes=vmem),
        name="merge_ffn",
    )(x, ya, yb, ga, gb, woh, woa, wout, n2, wg, wu, wd)


def kernel(x_prompt, x_sample, state_hgrn, cache_k, cache_v, norm1_w, w_in, lb_logits, hgrn_norm_w,
           q_norm_w, k_norm_w, sinks, w_o_hgrn, w_o_attn, w_out, norm2_w, w_ffn_gate, w_ffn_up,
           w_ffn_down):
    depth = w_in.shape[0]
    assert depth == 1 and lb_logits.shape[0] == 2, "single-layer step only"
    batch, seq, d = x_prompt.shape
    nseq, dec_seq, _ = x_sample.shape
    cw = cache_k.shape[2]
    assert seq % ROW_TILE == 0 and seq % CHUNK == 0 and seq >= WINDOW
    assert dec_seq <= SEQ_PAD and cw == WINDOW and w_ffn_gate.shape[2] % FFN_CHUNK == 0
    assert (nseq * SEQ_PAD) % ROW_TILE == 0 and nseq % SWA_SAMPLE_SEQS == 0

    win = w_in[0].astype(BF16)
    n1 = norm1_w[0][None, :]
    n2 = norm2_w[0][None, :]
    qn = jnp.tile(q_norm_w[0], N_Q_HEADS)[None, :] * (HEAD_DIM ** -0.5 * LOG2E)
    kn = jnp.tile(k_norm_w[0], N_KV_HEADS)[None, :]
    nw = hgrn_norm_w[0][None, :]
    head_of = jnp.arange(Q_W, dtype=jnp.int32) // HEAD_DIM
    hsel = jnp.where(head_of[:, None] == head_of[None, :], 1.0 / HEAD_DIM, 0.0).astype(BF16)
    sk = sinks[0].astype(F32)
    woh, woa, wout = w_o_hgrn[0].astype(BF16), w_o_attn[0].astype(BF16), w_out[0].astype(BF16)
    wg, wu, wd = w_ffn_gate[0].astype(BF16), w_ffn_up[0].astype(BF16), w_ffn_down[0].astype(BF16)

    xp = x_prompt.reshape(batch * seq, d)
    hq, f, hv, og, aq, ak, av, ga, gb = _proj(xp, n1, win, lb_logits, qn, kn, hsel)
    ya, yb, s_prompt, kt_prompt, vt_prompt = _mixers(sk, hq, f, hv, og, aq, ak, av, nw, batch, seq)
    y_prompt = _merge_ffn(xp, ya, yb, ga, gb, woh, woa, wout, n2, wg, wu, wd).reshape(batch, seq, d)

    def from_transposed(t):
        return t.reshape(t.shape[0], N_KV_HEADS, HEAD_DIM, t.shape[2]).transpose(0, 3, 1, 2)

    def to_transposed(c):
        return c.transpose(0, 2, 3, 1).reshape(c.shape[0], KV_W, c.shape[1])

    xs = jnp.pad(x_sample, ((0, 0), (0, SEQ_PAD - dec_seq), (0, 0))).reshape(nseq * SEQ_PAD, d)
    hq, f, hv, og, aq, ak, av, ga, gb = _proj(xs, n1, win, lb_logits, qn, kn, hsel)
    ya, s_sample = _hgrn_sample(hq, f, hv, og, nw, state_hgrn[0], dec_seq)
    yb, kt_sample, vt_sample = _swa_sample(
        sk, aq.reshape(nseq, SEQ_PAD, Q_W), ak.reshape(nseq, SEQ_PAD, KV_W),
        av.reshape(nseq, SEQ_PAD, KV_W), to_transposed(cache_k[0]), to_transposed(cache_v[0]),
        dec_seq)
    ys = _merge_ffn(xs, ya, yb.reshape(nseq * SEQ_PAD, Q_W), ga, gb, woh, woa, wout, n2, wg, wu, wd)
    y_sample = ys.reshape(nseq, SEQ_PAD, d)[:, :dec_seq]

    return (y_prompt, y_sample, s_prompt[None], from_transposed(kt_prompt)[None],
            from_transposed(vt_prompt)[None], s_sample[None], from_transposed(kt_sample)[None],
            from_transposed(vt_sample)[None])
```

```python
import functools
import math

import jax
import jax.numpy as jnp
from jax import lax
from jax.experimental import pallas as pl
from jax.experimental.pallas import tpu as pltpu

F32 = jnp.float32
BF16 = jnp.bfloat16

EPS = 1e-6
NEG_BIG = -1e30

HG_HEADS = 4
HG_DK = 128
HG_DV = 128
HG_W = HG_HEADS * HG_DK
N_Q_HEADS = 8
N_KV_HEADS = 2
Q_PER_KV = N_Q_HEADS // N_KV_HEADS
HEAD_DIM = 64
HEAD_SHIFT = HEAD_DIM.bit_length() - 1
WINDOW = 128
Q_W = N_Q_HEADS * HEAD_DIM
KV_W = N_KV_HEADS * HEAD_DIM
GROUP_W = Q_PER_KV * HEAD_DIM

SEQ_PAD = 8
ROW_TILE = 512
PROJ_ROWS = 1024
CHUNK = 128
FFN_CHUNK = 256
DOWN_TILE = 1024
MIX_SUB = 8
PROJ_TILE = 256
HEAD_GROUP = 2
HGRN_PIECES = MIX_SUB * 20
SWA_PIECES = MIX_SUB * 18
DIAG = 64
V7X_VMEM_BYTES = 64 * 1024 * 1024
SCOPED_VMEM_BYTES = V7X_VMEM_BYTES - 4 * 1024 * 1024


def _nbytes(shape, dtype):
    return math.prod(shape) * jnp.dtype(dtype).itemsize


def _vmem_limit(resident, streamed, scratch, temporaries):
    need = resident + 2 * streamed + scratch + temporaries
    assert need <= SCOPED_VMEM_BYTES, need
    return SCOPED_VMEM_BYTES


def _nt_dot(a, b):
    return lax.dot_general(a, b, (((1,), (1,)), ((), ())), preferred_element_type=F32)


def _tn_dot(a, b):
    return lax.dot_general(a, b, (((0,), (0,)), ((), ())), preferred_element_type=F32)


def _sigmoid(x):
    return 1.0 / (1.0 + jnp.exp(-x))


def _drain(gen):
    try:
        while True:
            next(gen)
    except StopIteration as stop:
        return stop.value


def _interleave(gens, pieces):
    done = [0] * len(gens)
    results = [None] * len(gens)
    live = list(range(len(gens)))
    while live:
        i = min(live, key=lambda g: ((done[g] + 1) / pieces[g], g))
        try:
            next(gens[i])
            done[i] += 1
        except StopIteration as stop:
            results[i] = stop.value
            live.remove(i)
    return results


def _const_spec(shape):
    nd = len(shape)
    return pl.BlockSpec(shape, lambda *_: (0,) * nd, pipeline_mode=pl.Buffered(1))


def _proj_rows(rows, x_ref, n1_ref, win_ref, lbl_ref, qn_ref, kn_ref, hsel_ref,
               hq_ref, f_ref, hv_ref, og_ref, aq_ref, ak_ref, av_ref, ga_ref, gb_ref):
    x = x_ref[rows, :]
    h = x * lax.rsqrt(jnp.mean(x * x, axis=-1, keepdims=True) + EPS) * n1_ref[...]
    hb = h.astype(BF16)
    yield

    w = PROJ_TILE
    col = [0]

    def proj():
        res = jnp.dot(hb, win_ref[:, col[0]:col[0] + w], preferred_element_type=F32)
        col[0] += w
        return res

    lg = lbl_ref[...]
    e = jnp.exp(lg - jnp.max(lg, axis=0, keepdims=True))
    lb = e[0:1, :] / jnp.sum(e, axis=0, keepdims=True)

    for c in range(0, HG_W, w):
        qa = proj()
        yield
        hq_ref[rows, c:c + w] = (qa * _sigmoid(qa) * (HG_DK ** -0.5)).astype(BF16)
        yield
    for c in range(0, HG_W, w):
        fa = proj()
        yield
        f_ref[rows, c:c + w] = lb[:, c:c + w] + (1.0 - lb[:, c:c + w]) * _sigmoid(fa)
        yield
    for c in range(0, HG_W, w):
        hv_ref[rows, c:c + w] = proj().astype(BF16)
        yield
    for c in range(0, HG_W, w):
        go = proj()
        yield
        og_ref[rows, c:c + w] = (go * _sigmoid(go)).astype(BF16)
        yield

    for c in range(0, Q_W, w):
        qb = proj()
        yield
        msq = jnp.dot((qb * qb).astype(BF16), hsel_ref[0:w, 0:w], preferred_element_type=F32)
        aq_ref[rows, c:c + w] = (qb * lax.rsqrt(msq + EPS) * qn_ref[:, c:c + w]).astype(BF16)
        yield
    assert 2 * KV_W == w
    kv = proj()
    yield
    kb = kv[:, 0:KV_W]
    msk = jnp.dot((kb * kb).astype(BF16), hsel_ref[0:KV_W, 0:KV_W], preferred_element_type=F32)
    ak_ref[rows, :] = kb * lax.rsqrt(msk + EPS) * kn_ref[...]
    av_ref[rows, :] = kv[:, KV_W:2 * KV_W]
    yield

    d = x.shape[-1]
    for gate_ref in (ga_ref, gb_ref):
        for c in range(0, d, w):
            g = proj()
            yield
            gate_ref[rows, c:c + w] = _sigmoid(g).astype(BF16)
            yield


def _proj_kernel(*refs):
    half = PROJ_ROWS // 2
    first = _proj_rows(slice(0, half), *refs)
    second = _proj_rows(slice(half, PROJ_ROWS), *refs)
    next(first)
    _interleave([first, second], [1, 1])


def _proj(x, n1, win, lbl, qn, kn, hsel):
    n, d = x.shape
    pw = win.shape[1]
    tm = PROJ_ROWS
    row = lambda w: pl.BlockSpec((tm, w), lambda i: (i, 0))
    out_shapes = (
        jax.ShapeDtypeStruct((n, HG_W), BF16),
        jax.ShapeDtypeStruct((n, HG_W), F32),
        jax.ShapeDtypeStruct((n, HG_W), BF16),
        jax.ShapeDtypeStruct((n, HG_W), BF16),
        jax.ShapeDtypeStruct((n, Q_W), BF16),
        jax.ShapeDtypeStruct((n, KV_W), F32),
        jax.ShapeDtypeStruct((n, KV_W), F32),
        jax.ShapeDtypeStruct((n, d), BF16),
        jax.ShapeDtypeStruct((n, d), BF16),
    )
    vmem = _vmem_limit(
        resident=_nbytes((d, pw), BF16) + _nbytes((Q_W, Q_W), BF16),
        streamed=_nbytes((tm, d), F32) + sum(_nbytes((tm, s.shape[1]), s.dtype) for s in out_shapes),
        scratch=0,
        temporaries=3 * _nbytes((tm, d), F32))
    return pl.pallas_call(
        _proj_kernel,
        grid=(n // tm,),
        in_specs=[row(d), _const_spec((1, d)), _const_spec((d, pw)), _const_spec(lbl.shape),
                  _const_spec((1, Q_W)), _const_spec((1, KV_W)), _const_spec((Q_W, Q_W))],
        out_specs=[row(HG_W), row(HG_W), row(HG_W), row(HG_W), row(Q_W), row(KV_W), row(KV_W),
                   row(d), row(d)],
        out_shape=out_shapes,
        compiler_params=pltpu.CompilerParams(
            dimension_semantics=("arbitrary",), vmem_limit_bytes=vmem),
        name="proj",
    )(x, n1, win, lbl, qn, kn, hsel)


def _roll_rows(x, d):
    d = d % x.shape[0]
    return x if d == 0 else pltpu.roll(x, d, axis=0)


SUBLANES = 8


def _slabs(x):
    return [x[r * SUBLANES:(r + 1) * SUBLANES, :] for r in range(x.shape[0] // SUBLANES)]


def _pair_levels(rows):
    ti = lax.broadcasted_iota(jnp.int32, (rows, rows), 0)
    tj = lax.broadcasted_iota(jnp.int32, (rows, rows), 1)
    lv = 31 - lax.clz(ti ^ tj)
    return jnp.where(ti > tj, lv, jnp.where(ti == tj, DIAG, -1))


def _hgrn_levels(qh, fh, lvl_slabs, nlev):
    ns = qh.shape[0] // SUBLANES
    kh = 1.0 - fh
    qs, ks = _slabs(qh), _slabs(kh)
    incl, tot, rexc = _slabs(fh), _slabs(fh), [None] * ns
    t8 = lax.broadcasted_iota(jnp.int32, (SUBLANES, qh.shape[1]), 0)
    zero = jnp.zeros((SUBLANES, qh.shape[1]), F32)
    qb, kb = qh.astype(BF16), kh.astype(BF16)
    diag = _slabs(_nt_dot(qb, kb))
    p = [jnp.where(lvl_slabs[r] == DIAG, diag[r], 0.0) for r in range(ns)]
    for m in range(nlev):
        h = 1 << m
        hs = h // SUBLANES
        upper = [hs == 0 or (r // hs) % 2 == 1 for r in range(ns)]
        lower = [hs == 0 or (r // hs) % 2 == 0 for r in range(ns)]
        a = jnp.concatenate([qs[r] * incl[r] if upper[r] else zero for r in range(ns)], axis=0)
        if m == 0:
            b = kb
        else:
            b = jnp.concatenate([ks[r] * rexc[r] if lower[r] else zero for r in range(ns)],
                                axis=0).astype(BF16)
        sc = _slabs(_nt_dot(a.astype(BF16), b))
        p = [jnp.where(lvl_slabs[r] == m, sc[r], p[r]) if upper[r] else p[r] for r in range(ns)]
        yield
        if hs == 0:
            up = (t8 & h) != 0
            for r in range(ns):
                below = pltpu.roll(tot[r], h, axis=0)
                above = pltpu.roll(tot[r], SUBLANES - h, axis=0)
                incl[r] = incl[r] * jnp.where(up, below, 1.0)
                grow = jnp.where(up, 1.0, above)
                rexc[r] = grow if rexc[r] is None else rexc[r] * grow
                tot[r] = tot[r] * jnp.where(up, below, above)
        else:
            for b0 in range(0, ns, 2 * hs):
                t_lo, t_hi = tot[b0], tot[b0 + hs]
                both = t_lo * t_hi
                for r in range(b0, b0 + hs):
                    rexc[r] = rexc[r] * t_hi
                    tot[r] = both
                for r in range(b0 + hs, b0 + 2 * hs):
                    incl[r] = incl[r] * t_lo
                    tot[r] = both
    return jnp.concatenate(p, axis=0), qs, ks, incl, rexc, tot


def _hgrn_finish(o, nw, og):
    return o * lax.rsqrt(jnp.mean(o * o, axis=-1, keepdims=True) + EPS) * nw * og


def _hgrn_chunk(q, f, v, og, nw, states):
    nlev = CHUNK.bit_length() - 1
    lvl_slabs = _slabs(_pair_levels(CHUNK))
    ys, new_states = [], []
    for h0 in range(0, HG_HEADS, HEAD_GROUP):
        group = range(h0, h0 + HEAD_GROUP)
        sweeps = {hd: _hgrn_levels(q[:, hd * HG_DK:(hd + 1) * HG_DK],
                                   f[:, hd * HG_DK:(hd + 1) * HG_DK], lvl_slabs, nlev)
                  for hd in group}
        swept = {}
        for _ in range(nlev + 1):
            for hd in group:
                try:
                    next(sweeps[hd])
                except StopIteration as stop:
                    swept[hd] = stop.value
            yield
        for hd in group:
            sl = slice(hd * HG_DK, (hd + 1) * HG_DK)
            p, qs, ks, incl, rexc, tot = swept[hd]
            ns = len(qs)
            qd = jnp.concatenate([qs[r] * incl[r] for r in range(ns)], axis=0).astype(BF16)
            kd = jnp.concatenate([ks[r] * rexc[r] for r in range(ns)], axis=0).astype(BF16)
            st = states[hd]
            o = (jnp.dot(p.astype(BF16), v[:, sl], preferred_element_type=F32)
                 + _nt_dot(qd, st.astype(BF16)))
            new_states.append(st * tot[0][0:1, :] + _tn_dot(v[:, sl], kd))
            ys.append(_hgrn_finish(o, nw[:, sl], og[:, sl]))
            yield
    return ys, new_states


def _hgrn_sample_kernel(hq_ref, f_ref, hv_ref, og_ref, nw_ref, s_ref, ya_ref, so_ref, *, seq_len):
    nlev = SEQ_PAD.bit_length() - 1
    t = lax.broadcasted_iota(jnp.int32, (CHUNK, HG_W), 0)
    f = jnp.where((t & (SEQ_PAD - 1)) < seq_len, f_ref[...], 1.0)
    q = hq_ref[...].astype(F32)
    v = hv_ref[...]
    vf = v.astype(F32)
    og = og_ref[...].astype(F32)
    nw = nw_ref[...]
    lvl_slabs = _slabs(_pair_levels(CHUNK))
    assert SEQ_PAD == SUBLANES
    for hd in range(HG_HEADS):
        sl = slice(hd * HG_DK, (hd + 1) * HG_DK)
        p, qs, ks, incl, rexc, tot = _drain(_hgrn_levels(q[:, sl], f[:, sl], lvl_slabs, nlev))
        o_inter = []
        for s in range(CHUNK // SEQ_PAD):
            rows = slice(s * SEQ_PAD, (s + 1) * SEQ_PAD)
            st = s_ref[s, hd]
            o_inter.append(jnp.dot(qs[s] * incl[s], st, preferred_element_type=F32))
            decay = jnp.broadcast_to(tot[s][0:1, :], (HG_DV, HG_DK)).T
            so_ref[s, hd] = st * decay + _tn_dot(ks[s] * rexc[s], vf[rows, sl])
        o = (jnp.dot(p.astype(BF16), v[:, sl], preferred_element_type=F32)
             + jnp.concatenate(o_inter, axis=0))
        ya_ref[:, sl] = _hgrn_finish(o, nw[:, sl], og[:, sl]).astype(BF16)


def _hgrn_sample(hq, f, hv, og, nw, state, seq_len):
    n = hq.shape[0]
    spc = CHUNK // SEQ_PAD
    blk = pl.BlockSpec((CHUNK, HG_W), lambda i: (i, 0))
    sblk = pl.BlockSpec((spc, HG_HEADS, HG_DK, HG_DV), lambda i: (i, 0, 0, 0))
    state_block = _nbytes((spc, HG_HEADS, HG_DK, HG_DV), F32)
    vmem = _vmem_limit(
        resident=0,
        streamed=2 * state_block + _nbytes((CHUNK, HG_W), F32) + 4 * _nbytes((CHUNK, HG_W), BF16),
        scratch=0,
        temporaries=16 * _nbytes((CHUNK, HG_W), F32))
    return pl.pallas_call(
        functools.partial(_hgrn_sample_kernel, seq_len=seq_len),
        grid=(n // CHUNK,),
        in_specs=[blk, blk, blk, blk, _const_spec((1, HG_W)), sblk],
        out_specs=[blk, sblk],
        out_shape=(jax.ShapeDtypeStruct((n, HG_W), BF16),
                   jax.ShapeDtypeStruct(state.shape, F32)),
        compiler_params=pltpu.CompilerParams(
            dimension_semantics=("arbitrary",), vmem_limit_bytes=vmem),
        name="hgrn_sample",
    )(hq, f, hv, og, nw, state)


LOG2E = 1.4426950408889634


def _alibi_slope(head):
    return LOG2E * 2.0 ** (-8.0 * (head + 1) / N_Q_HEADS)


def _tile_kv_head(x, g):
    lane = lax.broadcasted_iota(jnp.int32, x.shape, 1)
    other = pltpu.roll(x, HEAD_DIM, axis=1)
    mine = (lane >> HEAD_SHIFT) == g
    one = jnp.where(mine, x, other).astype(BF16)
    return jnp.concatenate([one] * (GROUP_W // KV_W), axis=1)


def _attn_group(qg, kt, vt, bias_of, sink_of):
    lane_head = lax.broadcasted_iota(jnp.int32, kt.shape, 1) >> HEAD_SHIFT
    zero = jnp.zeros_like(kt)
    vbd = jnp.concatenate([jnp.where(lane_head == hh, vt, zero) for hh in range(Q_PER_KV)], axis=0)
    tq = lax.broadcasted_iota(jnp.int32, (WINDOW, WINDOW), 0)
    tk = lax.broadcasted_iota(jnp.int32, (WINDOW, WINDOW), 1)
    in_cur = tk <= tq
    ps, invs = [], []
    for hh in range(Q_PER_KV):
        s2 = _nt_dot(qg, jnp.where(lane_head == hh, kt, zero))
        s = jnp.where(in_cur, s2[:, WINDOW:], s2[:, :WINDOW]) + bias_of(hh)
        sink = sink_of(hh) * LOG2E
        m = jnp.maximum(jnp.max(s, axis=-1, keepdims=True), sink)
        yield
        p = jnp.exp2(s - m)
        denom = jnp.sum(p, axis=-1, keepdims=True) + jnp.exp2(sink - m)
        ps.append(jnp.where(in_cur, 0.0, p).astype(BF16))
        ps.append(jnp.where(in_cur, p, 0.0).astype(BF16))
        invs.append(1.0 / denom)
        yield
    o = jnp.dot(jnp.concatenate(ps, axis=1), vbd, preferred_element_type=F32)
    out_head = lax.broadcasted_iota(jnp.int32, o.shape, 1) >> HEAD_SHIFT
    scale = invs[Q_PER_KV - 1]
    for hh in range(Q_PER_KV - 2, -1, -1):
        scale = jnp.where(out_head == hh, invs[hh], scale)
    yield
    return o * scale


def _init_band_bias(bias_ref):
    tq = lax.broadcasted_iota(jnp.int32, (WINDOW, WINDOW), 0)
    c = lax.broadcasted_iota(jnp.int32, (WINDOW, WINDOW), 1)
    in_cur = c <= tq
    dist = jnp.where(in_cur, tq - c, WINDOW + tq - c)
    for h in range(N_Q_HEADS):
        bias = -_alibi_slope(h) * dist.astype(F32)
        bias_ref[0, h] = bias
        bias_ref[1, h] = jnp.where(in_cur, bias, NEG_BIG)


def _swa_block(q, k2, v2, bias_ref, table, sink_ref):
    outs = []
    for g in range(N_KV_HEADS):
        kt = _tile_kv_head(k2, g)
        vt = _tile_kv_head(v2, g)
        o = yield from _attn_group(q[:, g * GROUP_W:(g + 1) * GROUP_W], kt, vt,
                                   lambda hh, g=g: bias_ref[table, g * Q_PER_KV + hh],
                                   lambda hh, g=g: sink_ref[g * Q_PER_KV + hh])
        outs.append(o)
    return outs


SWA_SAMPLE_SEQS = 16
SWA_SAMPLE_BATCH = 8
STACK = Q_PER_KV * SEQ_PAD


def _swa_sample_kernel(sink_ref, q_ref, kn_ref, vn_ref, ckt_ref, cvt_ref,
                       y_ref, ckot_ref, cvot_ref, bias_ref, sinkv_ref, *, seq_len):
    cw = ckt_ref.shape[2]

    @pl.when(pl.program_id(0) == 0)
    def _():
        rows = SWA_SAMPLE_BATCH * STACK
        r = lax.broadcasted_iota(jnp.int32, (rows, cw), 0)
        c = lax.broadcasted_iota(jnp.int32, (rows, cw), 1)
        tq = r & (SEQ_PAD - 1)
        hh = (r >> (SEQ_PAD.bit_length() - 1)) & (Q_PER_KV - 1)
        dist_c = cw + tq - c
        valid_c = (dist_c >= 0) & (dist_c < WINDOW)
        dist_n = tq - c
        valid_n = (dist_n >= 0) & (dist_n < WINDOW) & (c < seq_len)
        for g in range(N_KV_HEADS):
            slope = jnp.full((rows, cw), _alibi_slope(g * Q_PER_KV + Q_PER_KV - 1), F32)
            sink = jnp.full((rows, cw), sink_ref[g * Q_PER_KV + Q_PER_KV - 1] * LOG2E, F32)
            for h in range(Q_PER_KV - 2, -1, -1):
                slope = jnp.where(hh == h, _alibi_slope(g * Q_PER_KV + h), slope)
                sink = jnp.where(hh == h, sink_ref[g * Q_PER_KV + h] * LOG2E, sink)
            bias_ref[g, 0] = jnp.where(valid_c, -slope * dist_c.astype(F32), NEG_BIG)
            bias_ref[g, 1] = jnp.where(valid_n, -slope * dist_n.astype(F32), NEG_BIG)
            sinkv_ref[g] = sink

    lane = lax.broadcasted_iota(jnp.int32, (KV_W, cw), 1)
    keep = lane < cw - seq_len
    pad = jnp.zeros((cw - SEQ_PAD, KV_W), F32)

    nb = SWA_SAMPLE_BATCH
    shift = cw - seq_len

    def body(i, carry):
        seqs = [i * nb + j for j in range(nb)]
        q = [q_ref[s].astype(F32) for s in seqs]
        k_new = [jnp.concatenate([kn_ref[s], pad], axis=0) for s in seqs]
        v_new = [jnp.concatenate([vn_ref[s], pad], axis=0) for s in seqs]
        ckt = [ckt_ref[s] for s in seqs]
        cvt = [cvt_ref[s] for s in seqs]
        for g in range(N_KV_HEADS):
            ch = slice(g * HEAD_DIM, (g + 1) * HEAD_DIM)
            qs = [jnp.concatenate(
                [qj[:, (g * Q_PER_KV + h) * HEAD_DIM:(g * Q_PER_KV + h + 1) * HEAD_DIM]
                 for h in range(Q_PER_KV)], axis=0).astype(BF16) for qj in q]
            s_c = jnp.concatenate(
                [jnp.dot(qs[j], ckt[j][ch, :].astype(BF16), preferred_element_type=F32)
                 for j in range(nb)], axis=0) + bias_ref[g, 0]
            s_n = jnp.concatenate(
                [_nt_dot(qs[j], k_new[j][:, ch].astype(BF16)) for j in range(nb)],
                axis=0) + bias_ref[g, 1]
            sink = sinkv_ref[g][:, 0:1]
            m = jnp.maximum(jnp.maximum(jnp.max(s_c, axis=-1, keepdims=True),
                                        jnp.max(s_n, axis=-1, keepdims=True)), sink)
            p_c = jnp.exp2(s_c - m)
            p_n = jnp.exp2(s_n - m)
            denom = (jnp.sum(p_c, axis=-1, keepdims=True) + jnp.sum(p_n, axis=-1, keepdims=True)
                     + jnp.exp2(sink - m))
            inv = 1.0 / denom
            p_c, p_n = p_c.astype(BF16), p_n.astype(BF16)
            for j in range(nb):
                rows = slice(j * STACK, (j + 1) * STACK)
                o = (_nt_dot(p_c[rows, :], cvt[j][ch, :].astype(BF16))
                     + jnp.dot(p_n[rows, :], v_new[j][:, ch].astype(BF16),
                               preferred_element_type=F32)) * inv[rows, :]
                y = jnp.concatenate(
                    [o[h * SEQ_PAD:(h + 1) * SEQ_PAD, :] for h in range(Q_PER_KV)], axis=1)
                y_ref[seqs[j], :, g * GROUP_W:(g + 1) * GROUP_W] = y.astype(BF16)
        for j in range(nb):
            ckot_ref[seqs[j]] = jnp.where(keep, pltpu.roll(ckt[j], shift, axis=1),
                                          _roll_rows(k_new[j], shift).T)
            cvot_ref[seqs[j]] = jnp.where(keep, pltpu.roll(cvt[j], shift, axis=1),
                                          _roll_rows(v_new[j], shift).T)
        return carry

    lax.fori_loop(0, q_ref.shape[0] // nb, body, 0)


def _swa_sample(sinks, aq, ak, av, cache_kt, cache_vt, seq_len):
    nseq, _, cw = cache_kt.shape
    g = SWA_SAMPLE_SEQS
    blk = lambda r, w: pl.BlockSpec((g, r, w), lambda i: (i, 0, 0))
    rows = SWA_SAMPLE_BATCH * STACK
    table = _nbytes((N_KV_HEADS, rows, cw), F32)
    vmem = _vmem_limit(
        resident=0,
        streamed=(4 * _nbytes((g, KV_W, cw), F32) + 2 * _nbytes((g, SEQ_PAD, Q_W), BF16)
                  + 2 * _nbytes((g, SEQ_PAD, KV_W), F32)),
        scratch=3 * table,
        temporaries=(4 * SWA_SAMPLE_BATCH + 8) * _nbytes((KV_W, cw), F32))
    return pl.pallas_call(
        functools.partial(_swa_sample_kernel, seq_len=seq_len),
        grid=(nseq // g,),
        in_specs=[pl.BlockSpec(memory_space=pltpu.SMEM),
                  blk(SEQ_PAD, Q_W), blk(SEQ_PAD, KV_W), blk(SEQ_PAD, KV_W),
                  blk(KV_W, cw), blk(KV_W, cw)],
        out_specs=[blk(SEQ_PAD, Q_W), blk(KV_W, cw), blk(KV_W, cw)],
        out_shape=(jax.ShapeDtypeStruct((nseq, SEQ_PAD, Q_W), BF16),
                   jax.ShapeDtypeStruct((nseq, KV_W, cw), F32),
                   jax.ShapeDtypeStruct((nseq, KV_W, cw), F32)),
        scratch_shapes=[pltpu.VMEM((N_KV_HEADS, 2, rows, cw), F32),
                        pltpu.VMEM((N_KV_HEADS, rows, cw), F32)],
        compiler_params=pltpu.CompilerParams(
            dimension_semantics=("arbitrary",), vmem_limit_bytes=vmem),
        name="swa_sample",
    )(sinks, aq, ak, av, cache_kt, cache_vt)


def _merge_ffn_rows(x, ya, yb, ga, gb, woh_ref, woa_ref, wout_ref, n2_ref, wg_ref, wu_ref, wd_ref,
                    act_ref):
    branch_a = jnp.dot(ya, woh_ref[...], preferred_element_type=F32)
    yield
    branch_b = jnp.dot(yb, woa_ref[...], preferred_element_type=F32)
    yield
    merged = ga.astype(F32) * branch_a + gb.astype(F32) * branch_b
    x1 = x + jnp.dot(merged.astype(BF16), wout_ref[...], preferred_element_type=F32)
    yield
    h2 = (x1 * lax.rsqrt(jnp.mean(x1 * x1, axis=-1, keepdims=True) + EPS) * n2_ref[...]).astype(BF16)
    d_ff = wg_ref.shape[1]
    for c in range(d_ff // FFN_CHUNK):
        cols = slice(c * FFN_CHUNK, (c + 1) * FFN_CHUNK)
        gate = jnp.dot(h2, wg_ref[:, cols], preferred_element_type=F32)
        yield
        up = jnp.dot(h2, wu_ref[:, cols], preferred_element_type=F32)
        yield
        act_ref[:, cols] = (gate * _sigmoid(gate) * up).astype(BF16)
        yield
    act = act_ref[...]
    outs = []
    for n in range(x.shape[1] // DOWN_TILE):
        cols = slice(n * DOWN_TILE, (n + 1) * DOWN_TILE)
        outs.append(x1[:, cols] + jnp.dot(act, wd_ref[:, cols], preferred_element_type=F32))
        yield
    return jnp.concatenate(outs, axis=1)


def _merge_ffn_kernel(x_ref, ya_ref, yb_ref, ga_ref, gb_ref, woh_ref, woa_ref, wout_ref,
                      n2_ref, wg_ref, wu_ref, wd_ref, o_ref, act_s):
    o_ref[...] = _drain(_merge_ffn_rows(
        x_ref[...], ya_ref[...], yb_ref[...], ga_ref[...], gb_ref[...],
        woh_ref, woa_ref, wout_ref, n2_ref, wg_ref, wu_ref, wd_ref, act_s))


def _mixers_kernel(sink_ref, hq_ref, f_ref, hv_ref, og_ref, aq_ref, kp_ref, kc_ref, vp_ref, vc_ref,
                   nw_ref, ya_ref, yb_ref, sfin_ref, klast_ref, vlast_ref, st_ref, bias_ref, *, nb):
    s = pl.program_id(0)

    @pl.when(s == 0)
    def _():
        _init_band_bias(bias_ref)
        st_ref[...] = jnp.zeros_like(st_ref)

    pos = lax.rem(s, nb)
    first = pos == 0

    def recurrence():
        states = [jnp.where(first, 0.0, st_ref[hd]) for hd in range(HG_HEADS)]
        ys_all = []
        for sub in range(MIX_SUB):
            rows = slice(sub * CHUNK, (sub + 1) * CHUNK)
            ys, states = yield from _hgrn_chunk(
                hq_ref[rows, :].astype(F32), f_ref[rows, :], hv_ref[rows, :],
                og_ref[rows, :].astype(F32), nw_ref[...], states)
            ys_all.append(ys)
        return ys_all, states

    def attention():
        k_prev, v_prev = kp_ref[...], vp_ref[...]
        outs_all = []
        for sub in range(MIX_SUB):
            rows = slice(sub * CHUNK, (sub + 1) * CHUNK)
            k_cur, v_cur = kc_ref[rows, :], vc_ref[rows, :]
            table = jnp.where(first, 1, 0) if sub == 0 else 0
            outs = yield from _swa_block(
                aq_ref[rows, :], jnp.concatenate([k_prev, k_cur], axis=0),
                jnp.concatenate([v_prev, v_cur], axis=0), bias_ref, table, sink_ref)
            k_prev, v_prev = k_cur, v_cur
            outs_all.append(outs)
        return outs_all

    (ys_all, new_states), outs_all = _interleave(
        [recurrence(), attention()], [HGRN_PIECES, SWA_PIECES])

    for hd in range(HG_HEADS):
        st_ref[hd] = new_states[hd]
    for sub in range(MIX_SUB):
        rows = slice(sub * CHUNK, (sub + 1) * CHUNK)
        for hd in range(HG_HEADS):
            ya_ref[rows, hd * HG_DV:(hd + 1) * HG_DV] = ys_all[sub][hd].astype(BF16)
        for g in range(N_KV_HEADS):
            yb_ref[rows, g * GROUP_W:(g + 1) * GROUP_W] = outs_all[sub][g].astype(BF16)

    @pl.when(pos == nb - 1)
    def _():
        for hd in range(HG_HEADS):
            sfin_ref[0, hd] = st_ref[hd].T
        last = slice((MIX_SUB - 1) * CHUNK, MIX_SUB * CHUNK)
        klast_ref[0] = kc_ref[last, :].T
        vlast_ref[0] = vc_ref[last, :].T


def _mixers(sinks, hq, f, hv, og, aq, ak, av, nw, batch, seq):
    n = batch * seq
    rows = MIX_SUB * CHUNK
    nb = seq // rows
    cur = lambda w: pl.BlockSpec((rows, w), lambda s: (s, 0))
    prev = lambda w: pl.BlockSpec((CHUNK, w), lambda s: (jnp.maximum(MIX_SUB * s - 1, 0), 0))
    per_seq = lambda *shape: pl.BlockSpec((1,) + shape, lambda s: (s // nb,) + (0,) * len(shape))
    state = _nbytes((HG_HEADS, HG_DV, HG_DK), F32)
    bias = _nbytes((2, N_Q_HEADS, WINDOW, WINDOW), F32)
    vmem = _vmem_limit(
        resident=0,
        streamed=(_nbytes((rows, HG_W), F32) + 6 * _nbytes((rows, HG_W), BF16)
                  + 2 * _nbytes((rows + CHUNK, KV_W), F32) + state + 2 * _nbytes((KV_W, WINDOW), F32)),
        scratch=state + bias,
        temporaries=MIX_SUB * 10 * _nbytes((CHUNK, HG_W), F32))
    return pl.pallas_call(
        functools.partial(_mixers_kernel, nb=nb),
        grid=(batch * nb,),
        in_specs=[pl.BlockSpec(memory_space=pltpu.SMEM),
                  cur(HG_W), cur(HG_W), cur(HG_W), cur(HG_W),
                  cur(Q_W), prev(KV_W), cur(KV_W), prev(KV_W), cur(KV_W),
                  _const_spec((1, HG_W))],
        out_specs=[cur(HG_W), cur(Q_W), per_seq(HG_HEADS, HG_DK, HG_DV),
                   per_seq(KV_W, WINDOW), per_seq(KV_W, WINDOW)],
        out_shape=(jax.ShapeDtypeStruct((n, HG_W), BF16),
                   jax.ShapeDtypeStruct((n, Q_W), BF16),
                   jax.ShapeDtypeStruct((batch, HG_HEADS, HG_DK, HG_DV), F32),
                   jax.ShapeDtypeStruct((batch, KV_W, WINDOW), F32),
                   jax.ShapeDtypeStruct((batch, KV_W, WINDOW), F32)),
        scratch_shapes=[pltpu.VMEM((HG_HEADS, HG_DV, HG_DK), F32),
                        pltpu.VMEM((2, N_Q_HEADS, WINDOW, WINDOW), F32)],
        compiler_params=pltpu.CompilerParams(
            dimension_semantics=("arbitrary",), vmem_limit_bytes=vmem),
        name="mixers",
    )(sinks, hq, f, hv, og, aq, ak, ak, av, av, nw)


def _merge_ffn(x, ya, yb, ga, gb, woh, woa, wout, n2, wg, wu, wd):
    n, d = x.shape
    d_ff = wg.shape[1]
    tm = ROW_TILE
    row = lambda w: pl.BlockSpec((tm, w), lambda i: (i, 0))
    vmem = _vmem_limit(
        resident=_nbytes((HG_W + Q_W + d, d), BF16) + 3 * _nbytes((d, d_ff), BF16),
        streamed=2 * _nbytes((tm, d), F32) + _nbytes((tm, HG_W + Q_W + 2 * d), BF16),
        scratch=_nbytes((tm, d_ff), BF16),
        temporaries=5 * _nbytes((tm, d), F32))
    return pl.pallas_call(
        _merge_ffn_kernel,
        grid=(n // tm,),
        in_specs=[row(d), row(HG_W), row(Q_W), row(d), row(d),
                  _const_spec((HG_W, d)), _const_spec((Q_W, d)), _const_spec((d, d)),
                  _const_spec((1, d)), _const_spec((d, d_ff)), _const_spec((d, d_ff)),
                  _const_spec((d_ff, d))],
        out_specs=row(d),
        out_shape=jax.ShapeDtypeStruct((n, d), F32),
        scratch_shapes=[pltpu.VMEM((tm, d_ff), BF16)],
        compiler_params=pltpu.CompilerParams(
            dimension_semantics=("arbitrary",), vmem_limit_bytes=vmem),
        name="merge_ffn",
    )(x, ya, yb, ga, gb, woh, woa, wout, n2, wg, wu, wd)


def kernel(x_prompt, x_sample, state_hgrn, cache_k, cache_v, norm1_w, w_in, lb_logits, hgrn_norm_w,
           q_norm_w, k_norm_w, sinks, w_o_hgrn, w_o_attn, w_out, norm2_w, w_ffn_gate, w_ffn_up,
           w_ffn_down):
    depth = w_in.shape[0]
    assert depth == 1 and lb_logits.shape[0] == 2, "single-layer step only"
    batch, seq, d = x_prompt.shape
    nseq, dec_seq, _ = x_sample.shape
    cw = cache_k.shape[2]
    assert seq % ROW_TILE == 0 and seq % CHUNK == 0 and seq >= WINDOW
    assert dec_seq <= SEQ_PAD and cw == WINDOW and w_ffn_gate.shape[2] % FFN_CHUNK == 0
    assert (nseq * SEQ_PAD) % PROJ_ROWS == 0 and (batch * seq) % PROJ_ROWS == 0
    assert (nseq * SEQ_PAD) % ROW_TILE == 0 and nseq % SWA_SAMPLE_SEQS == 0

    win = w_in[0].astype(BF16)
    n1 = norm1_w[0][None, :]
    n2 = norm2_w[0][None, :]
    qn = jnp.tile(q_norm_w[0], N_Q_HEADS)[None, :] * (HEAD_DIM ** -0.5 * LOG2E)
    kn = jnp.tile(k_norm_w[0], N_KV_HEADS)[None, :]
    nw = hgrn_norm_w[0][None, :]
    head_of = jnp.arange(Q_W, dtype=jnp.int32) // HEAD_DIM
    hsel = jnp.where(head_of[:, None] == head_of[None, :], 1.0 / HEAD_DIM, 0.0).astype(BF16)
    sk = sinks[0].astype(F32)
    woh, woa, wout = w_o_hgrn[0].astype(BF16), w_o_attn[0].astype(BF16), w_out[0].astype(BF16)
    wg, wu, wd = w_ffn_gate[0].astype(BF16), w_ffn_up[0].astype(BF16), w_ffn_down[0].astype(BF16)

    xp = x_prompt.reshape(batch * seq, d)
    hq, f, hv, og, aq, ak, av, ga, gb = _proj(xp, n1, win, lb_logits, qn, kn, hsel)
    ya, yb, s_prompt, kt_prompt, vt_prompt = _mixers(sk, hq, f, hv, og, aq, ak, av, nw, batch, seq)
    y_prompt = _merge_ffn(xp, ya, yb, ga, gb, woh, woa, wout, n2, wg, wu, wd).reshape(batch, seq, d)

    def from_transposed(t):
        return t.reshape(t.shape[0], N_KV_HEADS, HEAD_DIM, t.shape[2]).transpose(0, 3, 1, 2)

    def to_transposed(c):
        return c.transpose(0, 2, 3, 1).reshape(c.shape[0], KV_W, c.shape[1])

    xs = jnp.pad(x_sample, ((0, 0), (0, SEQ_PAD - dec_seq), (0, 0))).reshape(nseq * SEQ_PAD, d)
    hq, f, hv, og, aq, ak, av, ga, gb = _proj(xs, n1, win, lb_logits, qn, kn, hsel)
    ya, s_sample = _hgrn_sample(hq, f, hv, og, nw, state_hgrn[0], dec_seq)
    yb, kt_sample, vt_sample = _swa_sample(
        sk, aq.reshape(nseq, SEQ_PAD, Q_W), ak.reshape(nseq, SEQ_PAD, KV_W),
        av.reshape(nseq, SEQ_PAD, KV_W), to_transposed(cache_k[0]), to_transposed(cache_v[0]),
        dec_seq)
    ys = _merge_ffn(xs, ya, yb.reshape(nseq * SEQ_PAD, Q_W), ga, gb, woh, woa, wout, n2, wg, wu, wd)
    y_sample = ys.reshape(nseq, SEQ_PAD, d)[:, :dec_seq]

    return (y_prompt, y_sample, s_prompt[None], from_transposed(kt_prompt)[None],
            from_transposed(vt_prompt)[None], s_sample[None], from_transposed(kt_sample)[None],
            from_transposed(vt_sample)[None])
```

```python
import functools
import math

import jax
import jax.numpy as jnp
from jax import lax
from jax.experimental import pallas as pl
from jax.experimental.pallas import tpu as pltpu

F32 = jnp.float32
BF16 = jnp.bfloat16

EPS = 1e-6
NEG_BIG = -1e30

HG_HEADS = 4
HG_DK = 128
HG_DV = 128
HG_W = HG_HEADS * HG_DK
N_Q_HEADS = 8
N_KV_HEADS = 2
Q_PER_KV = N_Q_HEADS // N_KV_HEADS
HEAD_DIM = 64
HEAD_SHIFT = HEAD_DIM.bit_length() - 1
WINDOW = 128
Q_W = N_Q_HEADS * HEAD_DIM
KV_W = N_KV_HEADS * HEAD_DIM
GROUP_W = Q_PER_KV * HEAD_DIM

SEQ_PAD = 8
ROW_TILE = 512
CHUNK = 128
FFN_CHUNK = 256
DOWN_TILE = 1024
MIX_SUB = 4
MERGE_SUB = 2
MERGE_PIECES = 3 * MIX_SUB // MERGE_SUB
PROJ_TILE = 256
HEAD_GROUP = 2
HGRN_PIECES = MIX_SUB * 20
SWA_PIECES = MIX_SUB * 18
DIAG = 64
V7X_VMEM_BYTES = 64 * 1024 * 1024
SCOPED_VMEM_BYTES = V7X_VMEM_BYTES - 4 * 1024 * 1024


def _nbytes(shape, dtype):
    return math.prod(shape) * jnp.dtype(dtype).itemsize


def _vmem_limit(resident, streamed, scratch, temporaries):
    need = resident + 2 * streamed + scratch + temporaries
    assert need <= SCOPED_VMEM_BYTES, need
    return SCOPED_VMEM_BYTES


def _nt_dot(a, b):
    return lax.dot_general(a, b, (((1,), (1,)), ((), ())), preferred_element_type=F32)


def _tn_dot(a, b):
    return lax.dot_general(a, b, (((0,), (0,)), ((), ())), preferred_element_type=F32)


def _sigmoid(x):
    return 1.0 / (1.0 + jnp.exp(-x))


def _drain(gen):
    try:
        while True:
            next(gen)
    except StopIteration as stop:
        return stop.value


WAIT = "wait"


def _interleave(gens, pieces):
    done = [0] * len(gens)
    results = [None] * len(gens)
    live = list(range(len(gens)))
    waiting = set()
    while live:
        ready = [g for g in live if g not in waiting]
        assert ready, "every generator is waiting"
        i = min(ready, key=lambda g: ((done[g] + 1) / pieces[g], g))
        try:
            if next(gens[i]) is WAIT:
                waiting.add(i)
                continue
            done[i] += 1
        except StopIteration as stop:
            results[i] = stop.value
            live.remove(i)
        waiting.clear()
    return results


def _const_spec(shape):
    nd = len(shape)
    return pl.BlockSpec(shape, lambda *_: (0,) * nd, pipeline_mode=pl.Buffered(1))


def _proj_rows(rows, x_ref, n1_ref, win_ref, lbl_ref, qn_ref, kn_ref, hsel_ref,
               hq_ref, f_ref, hv_ref, og_ref, aq_ref, ak_ref, av_ref, ga_ref, gb_ref):
    x = x_ref[rows, :]
    h = x * lax.rsqrt(jnp.mean(x * x, axis=-1, keepdims=True) + EPS) * n1_ref[...]
    hb = h.astype(BF16)
    yield

    w = PROJ_TILE
    col = [0]

    def proj():
        res = jnp.dot(hb, win_ref[:, col[0]:col[0] + w], preferred_element_type=F32)
        col[0] += w
        return res

    lg = lbl_ref[...]
    e = jnp.exp(lg - jnp.max(lg, axis=0, keepdims=True))
    lb = e[0:1, :] / jnp.sum(e, axis=0, keepdims=True)

    for c in range(0, HG_W, w):
        qa = proj()
        yield
        hq_ref[rows, c:c + w] = (qa * _sigmoid(qa) * (HG_DK ** -0.5)).astype(BF16)
        yield
    for c in range(0, HG_W, w):
        fa = proj()
        yield
        f_ref[rows, c:c + w] = lb[:, c:c + w] + (1.0 - lb[:, c:c + w]) * _sigmoid(fa)
        yield
    for c in range(0, HG_W, w):
        hv_ref[rows, c:c + w] = proj().astype(BF16)
        yield
    for c in range(0, HG_W, w):
        go = proj()
        yield
        og_ref[rows, c:c + w] = (go * _sigmoid(go)).astype(BF16)
        yield

    for c in range(0, Q_W, w):
        qb = proj()
        yield
        msq = jnp.dot((qb * qb).astype(BF16), hsel_ref[0:w, 0:w], preferred_element_type=F32)
        aq_ref[rows, c:c + w] = (qb * lax.rsqrt(msq + EPS) * qn_ref[:, c:c + w]).astype(BF16)
        yield
    assert 2 * KV_W == w
    kv = proj()
    yield
    kb = kv[:, 0:KV_W]
    msk = jnp.dot((kb * kb).astype(BF16), hsel_ref[0:KV_W, 0:KV_W], preferred_element_type=F32)
    ak_ref[rows, :] = kb * lax.rsqrt(msk + EPS) * kn_ref[...]
    av_ref[rows, :] = kv[:, KV_W:2 * KV_W]
    yield

    d = x.shape[-1]
    for gate_ref in (ga_ref, gb_ref):
        for c in range(0, d, w):
            g = proj()
            yield
            gate_ref[rows, c:c + w] = _sigmoid(g).astype(BF16)
            yield


def _proj_kernel(*refs):
    half = ROW_TILE // 2
    first = _proj_rows(slice(0, half), *refs)
    second = _proj_rows(slice(half, ROW_TILE), *refs)
    next(first)
    _interleave([first, second], [1, 1])


def _proj(x, n1, win, lbl, qn, kn, hsel):
    n, d = x.shape
    pw = win.shape[1]
    tm = ROW_TILE
    row = lambda w: pl.BlockSpec((tm, w), lambda i: (i, 0))
    out_shapes = (
        jax.ShapeDtypeStruct((n, HG_W), BF16),
        jax.ShapeDtypeStruct((n, HG_W), F32),
        jax.ShapeDtypeStruct((n, HG_W), BF16),
        jax.ShapeDtypeStruct((n, HG_W), BF16),
        jax.ShapeDtypeStruct((n, Q_W), BF16),
        jax.ShapeDtypeStruct((n, KV_W), F32),
        jax.ShapeDtypeStruct((n, KV_W), F32),
        jax.ShapeDtypeStruct((n, d), BF16),
        jax.ShapeDtypeStruct((n, d), BF16),
    )
    vmem = _vmem_limit(
        resident=_nbytes((d, pw), BF16) + _nbytes((Q_W, Q_W), BF16),
        streamed=_nbytes((tm, d), F32) + sum(_nbytes((tm, s.shape[1]), s.dtype) for s in out_shapes),
        scratch=0,
        temporaries=3 * _nbytes((tm, d), F32))
    return pl.pallas_call(
        _proj_kernel,
        grid=(n // tm,),
        in_specs=[row(d), _const_spec((1, d)), _const_spec((d, pw)), _const_spec(lbl.shape),
                  _const_spec((1, Q_W)), _const_spec((1, KV_W)), _const_spec((Q_W, Q_W))],
        out_specs=[row(HG_W), row(HG_W), row(HG_W), row(HG_W), row(Q_W), row(KV_W), row(KV_W),
                   row(d), row(d)],
        out_shape=out_shapes,
        compiler_params=pltpu.CompilerParams(
            dimension_semantics=("arbitrary",), vmem_limit_bytes=vmem),
        name="proj",
    )(x, n1, win, lbl, qn, kn, hsel)


def _roll_rows(x, d):
    d = d % x.shape[0]
    return x if d == 0 else pltpu.roll(x, d, axis=0)


SUBLANES = 8


def _slabs(x):
    return [x[r * SUBLANES:(r + 1) * SUBLANES, :] for r in range(x.shape[0] // SUBLANES)]


def _pair_levels(rows):
    ti = lax.broadcasted_iota(jnp.int32, (rows, rows), 0)
    tj = lax.broadcasted_iota(jnp.int32, (rows, rows), 1)
    lv = 31 - lax.clz(ti ^ tj)
    return jnp.where(ti > tj, lv, jnp.where(ti == tj, DIAG, -1))


def _hgrn_levels(qh, fh, lvl_slabs, nlev):
    ns = qh.shape[0] // SUBLANES
    kh = 1.0 - fh
    qs, ks = _slabs(qh), _slabs(kh)
    incl, tot, rexc = _slabs(fh), _slabs(fh), [None] * ns
    t8 = lax.broadcasted_iota(jnp.int32, (SUBLANES, qh.shape[1]), 0)
    zero = jnp.zeros((SUBLANES, qh.shape[1]), F32)
    qb, kb = qh.astype(BF16), kh.astype(BF16)
    diag = _slabs(_nt_dot(qb, kb))
    p = [jnp.where(lvl_slabs[r] == DIAG, diag[r], 0.0) for r in range(ns)]
    for m in range(nlev):
        h = 1 << m
        hs = h // SUBLANES
        upper = [hs == 0 or (r // hs) % 2 == 1 for r in range(ns)]
        lower = [hs == 0 or (r // hs) % 2 == 0 for r in range(ns)]
        a = jnp.concatenate([qs[r] * incl[r] if upper[r] else zero for r in range(ns)], axis=0)
        if m == 0:
            b = kb
        else:
            b = jnp.concatenate([ks[r] * rexc[r] if lower[r] else zero for r in range(ns)],
                                axis=0).astype(BF16)
        sc = _slabs(_nt_dot(a.astype(BF16), b))
        p = [jnp.where(lvl_slabs[r] == m, sc[r], p[r]) if upper[r] else p[r] for r in range(ns)]
        yield
        if hs == 0:
            up = (t8 & h) != 0
            for r in range(ns):
                below = pltpu.roll(tot[r], h, axis=0)
                above = pltpu.roll(tot[r], SUBLANES - h, axis=0)
                incl[r] = incl[r] * jnp.where(up, below, 1.0)
                grow = jnp.where(up, 1.0, above)
                rexc[r] = grow if rexc[r] is None else rexc[r] * grow
                tot[r] = tot[r] * jnp.where(up, below, above)
        else:
            for b0 in range(0, ns, 2 * hs):
                t_lo, t_hi = tot[b0], tot[b0 + hs]
                both = t_lo * t_hi
                for r in range(b0, b0 + hs):
                    rexc[r] = rexc[r] * t_hi
                    tot[r] = both
                for r in range(b0 + hs, b0 + 2 * hs):
                    incl[r] = incl[r] * t_lo
                    tot[r] = both
    return jnp.concatenate(p, axis=0), qs, ks, incl, rexc, tot


def _hgrn_finish(o, nw, og):
    return o * lax.rsqrt(jnp.mean(o * o, axis=-1, keepdims=True) + EPS) * nw * og


def _hgrn_chunk(q, f, v, og, nw, states):
    nlev = CHUNK.bit_length() - 1
    lvl_slabs = _slabs(_pair_levels(CHUNK))
    ys, new_states = [], []
    for h0 in range(0, HG_HEADS, HEAD_GROUP):
        group = range(h0, h0 + HEAD_GROUP)
        sweeps = {hd: _hgrn_levels(q[:, hd * HG_DK:(hd + 1) * HG_DK],
                                   f[:, hd * HG_DK:(hd + 1) * HG_DK], lvl_slabs, nlev)
                  for hd in group}
        swept = {}
        for _ in range(nlev + 1):
            for hd in group:
                try:
                    next(sweeps[hd])
                except StopIteration as stop:
                    swept[hd] = stop.value
            yield
        for hd in group:
            sl = slice(hd * HG_DK, (hd + 1) * HG_DK)
            p, qs, ks, incl, rexc, tot = swept[hd]
            ns = len(qs)
            qd = jnp.concatenate([qs[r] * incl[r] for r in range(ns)], axis=0).astype(BF16)
            kd = jnp.concatenate([ks[r] * rexc[r] for r in range(ns)], axis=0).astype(BF16)
            st = states[hd]
            o = (jnp.dot(p.astype(BF16), v[:, sl], preferred_element_type=F32)
                 + _nt_dot(qd, st.astype(BF16)))
            new_states.append(st * tot[0][0:1, :] + _tn_dot(v[:, sl], kd))
            ys.append(_hgrn_finish(o, nw[:, sl], og[:, sl]))
            yield
    return ys, new_states


def _hgrn_sample_kernel(hq_ref, f_ref, hv_ref, og_ref, nw_ref, s_ref, ya_ref, so_ref, *, seq_len):
    nlev = SEQ_PAD.bit_length() - 1
    t = lax.broadcasted_iota(jnp.int32, (CHUNK, HG_W), 0)
    f = jnp.where((t & (SEQ_PAD - 1)) < seq_len, f_ref[...], 1.0)
    q = hq_ref[...].astype(F32)
    v = hv_ref[...]
    vf = v.astype(F32)
    og = og_ref[...].astype(F32)
    nw = nw_ref[...]
    lvl_slabs = _slabs(_pair_levels(CHUNK))
    assert SEQ_PAD == SUBLANES
    for hd in range(HG_HEADS):
        sl = slice(hd * HG_DK, (hd + 1) * HG_DK)
        p, qs, ks, incl, rexc, tot = _drain(_hgrn_levels(q[:, sl], f[:, sl], lvl_slabs, nlev))
        o_inter = []
        for s in range(CHUNK // SEQ_PAD):
            rows = slice(s * SEQ_PAD, (s + 1) * SEQ_PAD)
            st = s_ref[s, hd]
            o_inter.append(jnp.dot(qs[s] * incl[s], st, preferred_element_type=F32))
            decay = jnp.broadcast_to(tot[s][0:1, :], (HG_DV, HG_DK)).T
            so_ref[s, hd] = st * decay + _tn_dot(ks[s] * rexc[s], vf[rows, sl])
        o = (jnp.dot(p.astype(BF16), v[:, sl], preferred_element_type=F32)
             + jnp.concatenate(o_inter, axis=0))
        ya_ref[:, sl] = _hgrn_finish(o, nw[:, sl], og[:, sl]).astype(BF16)


def _hgrn_sample(hq, f, hv, og, nw, state, seq_len):
    n = hq.shape[0]
    spc = CHUNK // SEQ_PAD
    blk = pl.BlockSpec((CHUNK, HG_W), lambda i: (i, 0))
    sblk = pl.BlockSpec((spc, HG_HEADS, HG_DK, HG_DV), lambda i: (i, 0, 0, 0))
    state_block = _nbytes((spc, HG_HEADS, HG_DK, HG_DV), F32)
    vmem = _vmem_limit(
        resident=0,
        streamed=2 * state_block + _nbytes((CHUNK, HG_W), F32) + 4 * _nbytes((CHUNK, HG_W), BF16),
        scratch=0,
        temporaries=16 * _nbytes((CHUNK, HG_W), F32))
    return pl.pallas_call(
        functools.partial(_hgrn_sample_kernel, seq_len=seq_len),
        grid=(n // CHUNK,),
        in_specs=[blk, blk, blk, blk, _const_spec((1, HG_W)), sblk],
        out_specs=[blk, sblk],
        out_shape=(jax.ShapeDtypeStruct((n, HG_W), BF16),
                   jax.ShapeDtypeStruct(state.shape, F32)),
        compiler_params=pltpu.CompilerParams(
            dimension_semantics=("arbitrary",), vmem_limit_bytes=vmem),
        name="hgrn_sample",
    )(hq, f, hv, og, nw, state)


LOG2E = 1.4426950408889634


def _alibi_slope(head):
    return LOG2E * 2.0 ** (-8.0 * (head + 1) / N_Q_HEADS)


def _tile_kv_head(x, g):
    lane = lax.broadcasted_iota(jnp.int32, x.shape, 1)
    other = pltpu.roll(x, HEAD_DIM, axis=1)
    mine = (lane >> HEAD_SHIFT) == g
    one = jnp.where(mine, x, other).astype(BF16)
    return jnp.concatenate([one] * (GROUP_W // KV_W), axis=1)


def _attn_group(qg, kt, vt, bias_of, sink_of):
    lane_head = lax.broadcasted_iota(jnp.int32, kt.shape, 1) >> HEAD_SHIFT
    zero = jnp.zeros_like(kt)
    vbd = jnp.concatenate([jnp.where(lane_head == hh, vt, zero) for hh in range(Q_PER_KV)], axis=0)
    tq = lax.broadcasted_iota(jnp.int32, (WINDOW, WINDOW), 0)
    tk = lax.broadcasted_iota(jnp.int32, (WINDOW, WINDOW), 1)
    in_cur = tk <= tq
    ps, invs = [], []
    for hh in range(Q_PER_KV):
        s2 = _nt_dot(qg, jnp.where(lane_head == hh, kt, zero))
        s = jnp.where(in_cur, s2[:, WINDOW:], s2[:, :WINDOW]) + bias_of(hh)
        sink = sink_of(hh) * LOG2E
        m = jnp.maximum(jnp.max(s, axis=-1, keepdims=True), sink)
        yield
        p = jnp.exp2(s - m)
        denom = jnp.sum(p, axis=-1, keepdims=True) + jnp.exp2(sink - m)
        ps.append(jnp.where(in_cur, 0.0, p).astype(BF16))
        ps.append(jnp.where(in_cur, p, 0.0).astype(BF16))
        invs.append(1.0 / denom)
        yield
    o = jnp.dot(jnp.concatenate(ps, axis=1), vbd, preferred_element_type=F32)
    out_head = lax.broadcasted_iota(jnp.int32, o.shape, 1) >> HEAD_SHIFT
    scale = invs[Q_PER_KV - 1]
    for hh in range(Q_PER_KV - 2, -1, -1):
        scale = jnp.where(out_head == hh, invs[hh], scale)
    yield
    return o * scale


def _init_band_bias(bias_ref):
    tq = lax.broadcasted_iota(jnp.int32, (WINDOW, WINDOW), 0)
    c = lax.broadcasted_iota(jnp.int32, (WINDOW, WINDOW), 1)
    in_cur = c <= tq
    dist = jnp.where(in_cur, tq - c, WINDOW + tq - c)
    for h in range(N_Q_HEADS):
        bias = -_alibi_slope(h) * dist.astype(F32)
        bias_ref[0, h] = bias
        bias_ref[1, h] = jnp.where(in_cur, bias, NEG_BIG)


def _swa_block(q, k2, v2, bias_ref, table, sink_ref):
    outs = []
    for g in range(N_KV_HEADS):
        kt = _tile_kv_head(k2, g)
        vt = _tile_kv_head(v2, g)
        o = yield from _attn_group(q[:, g * GROUP_W:(g + 1) * GROUP_W], kt, vt,
                                   lambda hh, g=g: bias_ref[table, g * Q_PER_KV + hh],
                                   lambda hh, g=g: sink_ref[g * Q_PER_KV + hh])
        outs.append(o)
    return outs


SWA_SAMPLE_SEQS = 16
SWA_SAMPLE_BATCH = 8
STACK = Q_PER_KV * SEQ_PAD


def _swa_sample_kernel(sink_ref, q_ref, kn_ref, vn_ref, ckt_ref, cvt_ref,
                       y_ref, ckot_ref, cvot_ref, bias_ref, sinkv_ref, *, seq_len):
    cw = ckt_ref.shape[2]

    @pl.when(pl.program_id(0) == 0)
    def _():
        rows = SWA_SAMPLE_BATCH * STACK
        r = lax.broadcasted_iota(jnp.int32, (rows, cw), 0)
        c = lax.broadcasted_iota(jnp.int32, (rows, cw), 1)
        tq = r & (SEQ_PAD - 1)
        hh = (r >> (SEQ_PAD.bit_length() - 1)) & (Q_PER_KV - 1)
        dist_c = cw + tq - c
        valid_c = (dist_c >= 0) & (dist_c < WINDOW)
        dist_n = tq - c
        valid_n = (dist_n >= 0) & (dist_n < WINDOW) & (c < seq_len)
        for g in range(N_KV_HEADS):
            slope = jnp.full((rows, cw), _alibi_slope(g * Q_PER_KV + Q_PER_KV - 1), F32)
            sink = jnp.full((rows, cw), sink_ref[g * Q_PER_KV + Q_PER_KV - 1] * LOG2E, F32)
            for h in range(Q_PER_KV - 2, -1, -1):
                slope = jnp.where(hh == h, _alibi_slope(g * Q_PER_KV + h), slope)
                sink = jnp.where(hh == h, sink_ref[g * Q_PER_KV + h] * LOG2E, sink)
            bias_ref[g, 0] = jnp.where(valid_c, -slope * dist_c.astype(F32), NEG_BIG)
            bias_ref[g, 1] = jnp.where(valid_n, -slope * dist_n.astype(F32), NEG_BIG)
            sinkv_ref[g] = sink

    lane = lax.broadcasted_iota(jnp.int32, (KV_W, cw), 1)
    keep = lane < cw - seq_len
    pad = jnp.zeros((cw - SEQ_PAD, KV_W), F32)

    nb = SWA_SAMPLE_BATCH
    shift = cw - seq_len

    def body(i, carry):
        seqs = [i * nb + j for j in range(nb)]
        q = [q_ref[s].astype(F32) for s in seqs]
        k_new = [jnp.concatenate([kn_ref[s], pad], axis=0) for s in seqs]
        v_new = [jnp.concatenate([vn_ref[s], pad], axis=0) for s in seqs]
        ckt = [ckt_ref[s] for s in seqs]
        cvt = [cvt_ref[s] for s in seqs]
        for g in range(N_KV_HEADS):
            ch = slice(g * HEAD_DIM, (g + 1) * HEAD_DIM)
            qs = [jnp.concatenate(
                [qj[:, (g * Q_PER_KV + h) * HEAD_DIM:(g * Q_PER_KV + h + 1) * HEAD_DIM]
                 for h in range(Q_PER_KV)], axis=0).astype(BF16) for qj in q]
            s_c = jnp.concatenate(
                [jnp.dot(qs[j], ckt[j][ch, :].astype(BF16), preferred_element_type=F32)
                 for j in range(nb)], axis=0) + bias_ref[g, 0]
            s_n = jnp.concatenate(
                [_nt_dot(qs[j], k_new[j][:, ch].astype(BF16)) for j in range(nb)],
                axis=0) + bias_ref[g, 1]
            sink = sinkv_ref[g][:, 0:1]
            m = jnp.maximum(jnp.maximum(jnp.max(s_c, axis=-1, keepdims=True),
                                        jnp.max(s_n, axis=-1, keepdims=True)), sink)
            p_c = jnp.exp2(s_c - m)
            p_n = jnp.exp2(s_n - m)
            denom = (jnp.sum(p_c, axis=-1, keepdims=True) + jnp.sum(p_n, axis=-1, keepdims=True)
                     + jnp.exp2(sink - m))
            inv = 1.0 / denom
            p_c, p_n = p_c.astype(BF16), p_n.astype(BF16)
            for j in range(nb):
                rows = slice(j * STACK, (j + 1) * STACK)
                o = (_nt_dot(p_c[rows, :], cvt[j][ch, :].astype(BF16))
                     + jnp.dot(p_n[rows, :], v_new[j][:, ch].astype(BF16),
                               preferred_element_type=F32)) * inv[rows, :]
                y = jnp.concatenate(
                    [o[h * SEQ_PAD:(h + 1) * SEQ_PAD, :] for h in range(Q_PER_KV)], axis=1)
                y_ref[seqs[j], :, g * GROUP_W:(g + 1) * GROUP_W] = y.astype(BF16)
        for j in range(nb):
            ckot_ref[seqs[j]] = jnp.where(keep, pltpu.roll(ckt[j], shift, axis=1),
                                          _roll_rows(k_new[j], shift).T)
            cvot_ref[seqs[j]] = jnp.where(keep, pltpu.roll(cvt[j], shift, axis=1),
                                          _roll_rows(v_new[j], shift).T)
        return carry

    lax.fori_loop(0, q_ref.shape[0] // nb, body, 0)


def _swa_sample(sinks, aq, ak, av, cache_kt, cache_vt, seq_len):
    nseq, _, cw = cache_kt.shape
    g = SWA_SAMPLE_SEQS
    blk = lambda r, w: pl.BlockSpec((g, r, w), lambda i: (i, 0, 0))
    rows = SWA_SAMPLE_BATCH * STACK
    table = _nbytes((N_KV_HEADS, rows, cw), F32)
    vmem = _vmem_limit(
        resident=0,
        streamed=(4 * _nbytes((g, KV_W, cw), F32) + 2 * _nbytes((g, SEQ_PAD, Q_W), BF16)
                  + 2 * _nbytes((g, SEQ_PAD, KV_W), F32)),
        scratch=3 * table,
        temporaries=(4 * SWA_SAMPLE_BATCH + 8) * _nbytes((KV_W, cw), F32))
    return pl.pallas_call(
        functools.partial(_swa_sample_kernel, seq_len=seq_len),
        grid=(nseq // g,),
        in_specs=[pl.BlockSpec(memory_space=pltpu.SMEM),
                  blk(SEQ_PAD, Q_W), blk(SEQ_PAD, KV_W), blk(SEQ_PAD, KV_W),
                  blk(KV_W, cw), blk(KV_W, cw)],
        out_specs=[blk(SEQ_PAD, Q_W), blk(KV_W, cw), blk(KV_W, cw)],
        out_shape=(jax.ShapeDtypeStruct((nseq, SEQ_PAD, Q_W), BF16),
                   jax.ShapeDtypeStruct((nseq, KV_W, cw), F32),
                   jax.ShapeDtypeStruct((nseq, KV_W, cw), F32)),
        scratch_shapes=[pltpu.VMEM((N_KV_HEADS, 2, rows, cw), F32),
                        pltpu.VMEM((N_KV_HEADS, rows, cw), F32)],
        compiler_params=pltpu.CompilerParams(
            dimension_semantics=("arbitrary",), vmem_limit_bytes=vmem),
        name="swa_sample",
    )(sinks, aq, ak, av, cache_kt, cache_vt)


def _merge_rows(x, ya, yb, ga, gb, woh_ref, woa_ref, wout_ref):
    branch_a = jnp.dot(ya, woh_ref[...], preferred_element_type=F32)
    yield
    branch_b = jnp.dot(yb, woa_ref[...], preferred_element_type=F32)
    yield
    merged = ga.astype(F32) * branch_a + gb.astype(F32) * branch_b
    x1 = x + jnp.dot(merged.astype(BF16), wout_ref[...], preferred_element_type=F32)
    yield
    return x1


def _merge_ffn_rows(x, ya, yb, ga, gb, woh_ref, woa_ref, wout_ref, n2_ref, wg_ref, wu_ref, wd_ref,
                    act_ref):
    x1 = yield from _merge_rows(x, ya, yb, ga, gb, woh_ref, woa_ref, wout_ref)
    return (yield from _ffn_rows(x1, n2_ref, wg_ref, wu_ref, wd_ref, act_ref))


def _ffn_rows(x1, n2_ref, wg_ref, wu_ref, wd_ref, act_ref):
    x = x1
    h2 = (x1 * lax.rsqrt(jnp.mean(x1 * x1, axis=-1, keepdims=True) + EPS) * n2_ref[...]).astype(BF16)
    d_ff = wg_ref.shape[1]
    for c in range(d_ff // FFN_CHUNK):
        cols = slice(c * FFN_CHUNK, (c + 1) * FFN_CHUNK)
        gate = jnp.dot(h2, wg_ref[:, cols], preferred_element_type=F32)
        yield
        up = jnp.dot(h2, wu_ref[:, cols], preferred_element_type=F32)
        yield
        act_ref[:, cols] = (gate * _sigmoid(gate) * up).astype(BF16)
        yield
    act = act_ref[...]
    outs = []
    for n in range(x.shape[1] // DOWN_TILE):
        cols = slice(n * DOWN_TILE, (n + 1) * DOWN_TILE)
        outs.append(x1[:, cols] + jnp.dot(act, wd_ref[:, cols], preferred_element_type=F32))
        yield
    return jnp.concatenate(outs, axis=1)


def _merge_ffn_kernel(x_ref, ya_ref, yb_ref, ga_ref, gb_ref, woh_ref, woa_ref, wout_ref,
                      n2_ref, wg_ref, wu_ref, wd_ref, o_ref, act_s):
    o_ref[...] = _drain(_merge_ffn_rows(
        x_ref[...], ya_ref[...], yb_ref[...], ga_ref[...], gb_ref[...],
        woh_ref, woa_ref, wout_ref, n2_ref, wg_ref, wu_ref, wd_ref, act_s))


def _mixers_kernel(sink_ref, hq_ref, f_ref, hv_ref, og_ref, aq_ref, kp_ref, kc_ref, vp_ref, vc_ref,
                   nw_ref, x_ref, ga_ref, gb_ref, woh_ref, woa_ref, wout_ref,
                   x1_ref, sfin_ref, klast_ref, vlast_ref, st_ref, bias_ref, *, nb):
    s = pl.program_id(0)

    @pl.when(s == 0)
    def _():
        _init_band_bias(bias_ref)
        st_ref[...] = jnp.zeros_like(st_ref)

    pos = lax.rem(s, nb)
    first = pos == 0

    ys_all, outs_all = [], []

    def recurrence():
        states = [jnp.where(first, 0.0, st_ref[hd]) for hd in range(HG_HEADS)]
        for sub in range(MIX_SUB):
            rows = slice(sub * CHUNK, (sub + 1) * CHUNK)
            ys, states = yield from _hgrn_chunk(
                hq_ref[rows, :].astype(F32), f_ref[rows, :], hv_ref[rows, :],
                og_ref[rows, :].astype(F32), nw_ref[...], states)
            ys_all.append(jnp.concatenate(ys, axis=1).astype(BF16))
        return states

    def attention():
        k_prev, v_prev = kp_ref[...], vp_ref[...]
        for sub in range(MIX_SUB):
            rows = slice(sub * CHUNK, (sub + 1) * CHUNK)
            k_cur, v_cur = kc_ref[rows, :], vc_ref[rows, :]
            table = jnp.where(first, 1, 0) if sub == 0 else 0
            outs = yield from _swa_block(
                aq_ref[rows, :], jnp.concatenate([k_prev, k_cur], axis=0),
                jnp.concatenate([v_prev, v_cur], axis=0), bias_ref, table, sink_ref)
            k_prev, v_prev = k_cur, v_cur
            outs_all.append(jnp.concatenate(outs, axis=1).astype(BF16))

    def merge():
        for pair in range(MIX_SUB // MERGE_SUB):
            ready = MERGE_SUB * (pair + 1)
            while len(ys_all) < ready or len(outs_all) < ready:
                yield WAIT
            rows = slice(pair * MERGE_SUB * CHUNK, ready * CHUNK)
            x1_ref[rows, :] = yield from _merge_rows(
                x_ref[rows, :], jnp.concatenate(ys_all[ready - MERGE_SUB:ready], axis=0),
                jnp.concatenate(outs_all[ready - MERGE_SUB:ready], axis=0),
                ga_ref[rows, :], gb_ref[rows, :], woh_ref, woa_ref, wout_ref)

    new_states, _, _ = _interleave(
        [recurrence(), attention(), merge()], [HGRN_PIECES, SWA_PIECES, MERGE_PIECES])
    for hd in range(HG_HEADS):
        st_ref[hd] = new_states[hd]

    @pl.when(pos == nb - 1)
    def _():
        for hd in range(HG_HEADS):
            sfin_ref[0, hd] = st_ref[hd].T
        last = slice((MIX_SUB - 1) * CHUNK, MIX_SUB * CHUNK)
        klast_ref[0] = kc_ref[last, :].T
        vlast_ref[0] = vc_ref[last, :].T


def _mixers(sinks, hq, f, hv, og, aq, ak, av, nw, x, ga, gb, woh, woa, wout, batch, seq):
    n, d = x.shape
    rows = MIX_SUB * CHUNK
    nb = seq // rows
    cur = lambda w: pl.BlockSpec((rows, w), lambda s: (s, 0))
    prev = lambda w: pl.BlockSpec((CHUNK, w), lambda s: (jnp.maximum(MIX_SUB * s - 1, 0), 0))
    per_seq = lambda *shape: pl.BlockSpec((1,) + shape, lambda s: (s // nb,) + (0,) * len(shape))
    state = _nbytes((HG_HEADS, HG_DV, HG_DK), F32)
    bias = _nbytes((2, N_Q_HEADS, WINDOW, WINDOW), F32)
    vmem = _vmem_limit(
        resident=_nbytes((HG_W + Q_W + d, d), BF16),
        streamed=(_nbytes((rows, HG_W), F32) + 4 * _nbytes((rows, HG_W), BF16)
                  + 2 * _nbytes((rows, d), F32) + 2 * _nbytes((rows, d), BF16)
                  + 2 * _nbytes((rows + CHUNK, KV_W), F32) + state + 2 * _nbytes((KV_W, WINDOW), F32)),
        scratch=state + bias,
        temporaries=MIX_SUB * 10 * _nbytes((CHUNK, HG_W), F32))
    return pl.pallas_call(
        functools.partial(_mixers_kernel, nb=nb),
        grid=(batch * nb,),
        in_specs=[pl.BlockSpec(memory_space=pltpu.SMEM),
                  cur(HG_W), cur(HG_W), cur(HG_W), cur(HG_W),
                  cur(Q_W), prev(KV_W), cur(KV_W), prev(KV_W), cur(KV_W),
                  _const_spec((1, HG_W)), cur(d), cur(d), cur(d),
                  _const_spec((HG_W, d)), _const_spec((Q_W, d)), _const_spec((d, d))],
        out_specs=[cur(d), per_seq(HG_HEADS, HG_DK, HG_DV),
                   per_seq(KV_W, WINDOW), per_seq(KV_W, WINDOW)],
        out_shape=(jax.ShapeDtypeStruct((n, d), F32),
                   jax.ShapeDtypeStruct((batch, HG_HEADS, HG_DK, HG_DV), F32),
                   jax.ShapeDtypeStruct((batch, KV_W, WINDOW), F32),
                   jax.ShapeDtypeStruct((batch, KV_W, WINDOW), F32)),
        scratch_shapes=[pltpu.VMEM((HG_HEADS, HG_DV, HG_DK), F32),
                        pltpu.VMEM((2, N_Q_HEADS, WINDOW, WINDOW), F32)],
        compiler_params=pltpu.CompilerParams(
            dimension_semantics=("arbitrary",), vmem_limit_bytes=vmem),
        name="mixers",
    )(sinks, hq, f, hv, og, aq, ak, ak, av, av, nw, x, ga, gb, woh, woa, wout)


def _ffn_kernel(x1_ref, n2_ref, wg_ref, wu_ref, wd_ref, o_ref, act_s):
    o_ref[...] = _drain(_ffn_rows(x1_ref[...], n2_ref, wg_ref, wu_ref, wd_ref, act_s))


def _ffn(x1, n2, wg, wu, wd):
    n, d = x1.shape
    d_ff = wg.shape[1]
    tm = ROW_TILE
    row = pl.BlockSpec((tm, d), lambda i: (i, 0))
    vmem = _vmem_limit(
        resident=3 * _nbytes((d, d_ff), BF16),
        streamed=2 * _nbytes((tm, d), F32),
        scratch=_nbytes((tm, d_ff), BF16),
        temporaries=3 * _nbytes((tm, d), F32))
    return pl.pallas_call(
        _ffn_kernel,
        grid=(n // tm,),
        in_specs=[row, _const_spec((1, d)), _const_spec((d, d_ff)), _const_spec((d, d_ff)),
                  _const_spec((d_ff, d))],
        out_specs=row,
        out_shape=jax.ShapeDtypeStruct((n, d), F32),
        scratch_shapes=[pltpu.VMEM((tm, d_ff), BF16)],
        compiler_params=pltpu.CompilerParams(
            dimension_semantics=("arbitrary",), vmem_limit_bytes=vmem),
        name="ffn",
    )(x1, n2, wg, wu, wd)


def _merge_ffn(x, ya, yb, ga, gb, woh, woa, wout, n2, wg, wu, wd):
    n, d = x.shape
    d_ff = wg.shape[1]
    tm = ROW_TILE
    row = lambda w: pl.BlockSpec((tm, w), lambda i: (i, 0))
    vmem = _vmem_limit(
        resident=_nbytes((HG_W + Q_W + d, d), BF16) + 3 * _nbytes((d, d_ff), BF16),
        streamed=2 * _nbytes((tm, d), F32) + _nbytes((tm, HG_W + Q_W + 2 * d), BF16),
        scratch=_nbytes((tm, d_ff), BF16),
        temporaries=5 * _nbytes((tm, d), F32))
    return pl.pallas_call(
        _merge_ffn_kernel,
        grid=(n // tm,),
        in_specs=[row(d), row(HG_W), row(Q_W), row(d), row(d),
                  _const_spec((HG_W, d)), _const_spec((Q_W, d)), _const_spec((d, d)),
                  _const_spec((1, d)), _const_spec((d, d_ff)), _const_spec((d, d_ff)),
                  _const_spec((d_ff, d))],
        out_specs=row(d),
        out_shape=jax.ShapeDtypeStruct((n, d), F32),
        scratch_shapes=[pltpu.VMEM((tm, d_ff), BF16)],
        compiler_params=pltpu.CompilerParams(
            dimension_semantics=("arbitrary",), vmem_limit_bytes=vmem),
        name="merge_ffn",
    )(x, ya, yb, ga, gb, woh, woa, wout, n2, wg, wu, wd)


def kernel(x_prompt, x_sample, state_hgrn, cache_k, cache_v, norm1_w, w_in, lb_logits, hgrn_norm_w,
           q_norm_w, k_norm_w, sinks, w_o_hgrn, w_o_attn, w_out, norm2_w, w_ffn_gate, w_ffn_up,
           w_ffn_down):
    depth = w_in.shape[0]
    assert depth == 1 and lb_logits.shape[0] == 2, "single-layer step only"
    batch, seq, d = x_prompt.shape
    nseq, dec_seq, _ = x_sample.shape
    cw = cache_k.shape[2]
    assert seq % ROW_TILE == 0 and seq % CHUNK == 0 and seq >= WINDOW
    assert dec_seq <= SEQ_PAD and cw == WINDOW and w_ffn_gate.shape[2] % FFN_CHUNK == 0
    assert (nseq * SEQ_PAD) % ROW_TILE == 0 and nseq % SWA_SAMPLE_SEQS == 0

    win = w_in[0].astype(BF16)
    n1 = norm1_w[0][None, :]
    n2 = norm2_w[0][None, :]
    qn = jnp.tile(q_norm_w[0], N_Q_HEADS)[None, :] * (HEAD_DIM ** -0.5 * LOG2E)
    kn = jnp.tile(k_norm_w[0], N_KV_HEADS)[None, :]
    nw = hgrn_norm_w[0][None, :]
    head_of = jnp.arange(Q_W, dtype=jnp.int32) // HEAD_DIM
    hsel = jnp.where(head_of[:, None] == head_of[None, :], 1.0 / HEAD_DIM, 0.0).astype(BF16)
    sk = sinks[0].astype(F32)
    woh, woa, wout = w_o_hgrn[0].astype(BF16), w_o_attn[0].astype(BF16), w_out[0].astype(BF16)
    wg, wu, wd = w_ffn_gate[0].astype(BF16), w_ffn_up[0].astype(BF16), w_ffn_down[0].astype(BF16)

    xp = x_prompt.reshape(batch * seq, d)
    hq, f, hv, og, aq, ak, av, ga, gb = _proj(xp, n1, win, lb_logits, qn, kn, hsel)
    x1, s_prompt, kt_prompt, vt_prompt = _mixers(sk, hq, f, hv, og, aq, ak, av, nw, xp, ga, gb,
                                                 woh, woa, wout, batch, seq)
    y_prompt = _ffn(x1, n2, wg, wu, wd).reshape(batch, seq, d)

    def from_transposed(t):
        return t.reshape(t.shape[0], N_KV_HEADS, HEAD_DIM, t.shape[2]).transpose(0, 3, 1, 2)

    def to_transposed(c):
        return c.transpose(0, 2, 3, 1).reshape(c.shape[0], KV_W, c.shape[1])

    xs = jnp.pad(x_sample, ((0, 0), (0, SEQ_PAD - dec_seq), (0, 0))).reshape(nseq * SEQ_PAD, d)
    hq, f, hv, og, aq, ak, av, ga, gb = _proj(xs, n1, win, lb_logits, qn, kn, hsel)
    ya, s_sample = _hgrn_sample(hq, f, hv, og, nw, state_hgrn[0], dec_seq)
    yb, kt_sample, vt_sample = _swa_sample(
        sk, aq.reshape(nseq, SEQ_PAD, Q_W), ak.reshape(nseq, SEQ_PAD, KV_W),
        av.reshape(nseq, SEQ_PAD, KV_W), to_transposed(cache_k[0]), to_transposed(cache_v[0]),
        dec_seq)
    ys = _merge_ffn(xs, ya, yb.reshape(nseq * SEQ_PAD, Q_W), ga, gb, woh, woa, wout, n2, wg, wu, wd)
    y_sample = ys.reshape(nseq, SEQ_PAD, d)[:, :dec_seq]

    return (y_prompt, y_sample, s_prompt[None], from_transposed(kt_prompt)[None],
            from_transposed(vt_prompt)[None], s_sample[None], from_transposed(kt_sample)[None],
            from_transposed(vt_sample)[None])
```

```python
import functools
import math

import jax
import jax.numpy as jnp
from jax import lax
from jax.experimental import pallas as pl
from jax.experimental.pallas import tpu as pltpu

F32 = jnp.float32
BF16 = jnp.bfloat16

EPS = 1e-6
NEG_BIG = -1e30

HG_HEADS = 4
HG_DK = 128
HG_DV = 128
HG_W = HG_HEADS * HG_DK
N_Q_HEADS = 8
N_KV_HEADS = 2
Q_PER_KV = N_Q_HEADS // N_KV_HEADS
HEAD_DIM = 64
HEAD_SHIFT = HEAD_DIM.bit_length() - 1
WINDOW = 128
Q_W = N_Q_HEADS * HEAD_DIM
KV_W = N_KV_HEADS * HEAD_DIM
GROUP_W = Q_PER_KV * HEAD_DIM

SEQ_PAD = 8
ROW_TILE = 512
FFN_ROWS = 1024
CHUNK = 128
FFN_CHUNK = 256
DOWN_TILE = 1024
MIX_SUB = 4
MERGE_SUB = 2
MERGE_PIECES = 3 * MIX_SUB // MERGE_SUB
PROJ_TILE = 256
HEAD_GROUP = 2
HGRN_PIECES = MIX_SUB * 20
SWA_PIECES = MIX_SUB * 18
DIAG = 64
V7X_VMEM_BYTES = 64 * 1024 * 1024
SCOPED_VMEM_BYTES = V7X_VMEM_BYTES - 4 * 1024 * 1024


def _nbytes(shape, dtype):
    return math.prod(shape) * jnp.dtype(dtype).itemsize


def _vmem_limit(resident, streamed, scratch, temporaries):
    need = resident + 2 * streamed + scratch + temporaries
    assert need <= SCOPED_VMEM_BYTES, need
    return SCOPED_VMEM_BYTES


def _nt_dot(a, b):
    return lax.dot_general(a, b, (((1,), (1,)), ((), ())), preferred_element_type=F32)


def _tn_dot(a, b):
    return lax.dot_general(a, b, (((0,), (0,)), ((), ())), preferred_element_type=F32)


def _sigmoid(x):
    return 1.0 / (1.0 + jnp.exp(-x))


def _drain(gen):
    try:
        while True:
            next(gen)
    except StopIteration as stop:
        return stop.value


WAIT = "wait"


def _interleave(gens, pieces):
    done = [0] * len(gens)
    results = [None] * len(gens)
    live = list(range(len(gens)))
    waiting = set()
    while live:
        ready = [g for g in live if g not in waiting]
        assert ready, "every generator is waiting"
        i = min(ready, key=lambda g: ((done[g] + 1) / pieces[g], g))
        try:
            if next(gens[i]) is WAIT:
                waiting.add(i)
                continue
            done[i] += 1
        except StopIteration as stop:
            results[i] = stop.value
            live.remove(i)
        waiting.clear()
    return results


def _const_spec(shape):
    nd = len(shape)
    return pl.BlockSpec(shape, lambda *_: (0,) * nd, pipeline_mode=pl.Buffered(1))


def _proj_rows(rows, x_ref, n1_ref, win_ref, lbl_ref, qn_ref, kn_ref, hsel_ref,
               hq_ref, f_ref, hv_ref, og_ref, aq_ref, ak_ref, av_ref, ga_ref, gb_ref):
    x = x_ref[rows, :]
    h = x * lax.rsqrt(jnp.mean(x * x, axis=-1, keepdims=True) + EPS) * n1_ref[...]
    hb = h.astype(BF16)
    yield

    w = PROJ_TILE
    col = [0]

    def proj():
        res = jnp.dot(hb, win_ref[:, col[0]:col[0] + w], preferred_element_type=F32)
        col[0] += w
        return res

    lg = lbl_ref[...]
    e = jnp.exp(lg - jnp.max(lg, axis=0, keepdims=True))
    lb = e[0:1, :] / jnp.sum(e, axis=0, keepdims=True)

    for c in range(0, HG_W, w):
        qa = proj()
        yield
        hq_ref[rows, c:c + w] = (qa * _sigmoid(qa) * (HG_DK ** -0.5)).astype(BF16)
        yield
    for c in range(0, HG_W, w):
        fa = proj()
        yield
        f_ref[rows, c:c + w] = lb[:, c:c + w] + (1.0 - lb[:, c:c + w]) * _sigmoid(fa)
        yield
    for c in range(0, HG_W, w):
        hv_ref[rows, c:c + w] = proj().astype(BF16)
        yield
    for c in range(0, HG_W, w):
        go = proj()
        yield
        og_ref[rows, c:c + w] = (go * _sigmoid(go)).astype(BF16)
        yield

    for c in range(0, Q_W, w):
        qb = proj()
        yield
        msq = jnp.dot((qb * qb).astype(BF16), hsel_ref[0:w, 0:w], preferred_element_type=F32)
        aq_ref[rows, c:c + w] = (qb * lax.rsqrt(msq + EPS) * qn_ref[:, c:c + w]).astype(BF16)
        yield
    assert 2 * KV_W == w
    kv = proj()
    yield
    kb = kv[:, 0:KV_W]
    msk = jnp.dot((kb * kb).astype(BF16), hsel_ref[0:KV_W, 0:KV_W], preferred_element_type=F32)
    ak_ref[rows, :] = kb * lax.rsqrt(msk + EPS) * kn_ref[...]
    av_ref[rows, :] = kv[:, KV_W:2 * KV_W]
    yield

    d = x.shape[-1]
    for gate_ref in (ga_ref, gb_ref):
        for c in range(0, d, w):
            g = proj()
            yield
            gate_ref[rows, c:c + w] = _sigmoid(g).astype(BF16)
            yield


def _proj_kernel(*refs):
    half = ROW_TILE // 2
    first = _proj_rows(slice(0, half), *refs)
    second = _proj_rows(slice(half, ROW_TILE), *refs)
    next(first)
    _interleave([first, second], [1, 1])


def _proj(x, n1, win, lbl, qn, kn, hsel):
    n, d = x.shape
    pw = win.shape[1]
    tm = ROW_TILE
    row = lambda w: pl.BlockSpec((tm, w), lambda i: (i, 0))
    out_shapes = (
        jax.ShapeDtypeStruct((n, HG_W), BF16),
        jax.ShapeDtypeStruct((n, HG_W), F32),
        jax.ShapeDtypeStruct((n, HG_W), BF16),
        jax.ShapeDtypeStruct((n, HG_W), BF16),
        jax.ShapeDtypeStruct((n, Q_W), BF16),
        jax.ShapeDtypeStruct((n, KV_W), F32),
        jax.ShapeDtypeStruct((n, KV_W), F32),
        jax.ShapeDtypeStruct((n, d), BF16),
        jax.ShapeDtypeStruct((n, d), BF16),
    )
    vmem = _vmem_limit(
        resident=_nbytes((d, pw), BF16) + _nbytes((Q_W, Q_W), BF16),
        streamed=_nbytes((tm, d), F32) + sum(_nbytes((tm, s.shape[1]), s.dtype) for s in out_shapes),
        scratch=0,
        temporaries=3 * _nbytes((tm, d), F32))
    return pl.pallas_call(
        _proj_kernel,
        grid=(n // tm,),
        in_specs=[row(d), _const_spec((1, d)), _const_spec((d, pw)), _const_spec(lbl.shape),
                  _const_spec((1, Q_W)), _const_spec((1, KV_W)), _const_spec((Q_W, Q_W))],
        out_specs=[row(HG_W), row(HG_W), row(HG_W), row(HG_W), row(Q_W), row(KV_W), row(KV_W),
                   row(d), row(d)],
        out_shape=out_shapes,
        compiler_params=pltpu.CompilerParams(
            dimension_semantics=("arbitrary",), vmem_limit_bytes=vmem),
        name="proj",
    )(x, n1, win, lbl, qn, kn, hsel)


def _roll_rows(x, d):
    d = d % x.shape[0]
    return x if d == 0 else pltpu.roll(x, d, axis=0)


SUBLANES = 8


def _slabs(x):
    return [x[r * SUBLANES:(r + 1) * SUBLANES, :] for r in range(x.shape[0] // SUBLANES)]


def _pair_levels(rows):
    ti = lax.broadcasted_iota(jnp.int32, (rows, rows), 0)
    tj = lax.broadcasted_iota(jnp.int32, (rows, rows), 1)
    lv = 31 - lax.clz(ti ^ tj)
    return jnp.where(ti > tj, lv, jnp.where(ti == tj, DIAG, -1))


def _hgrn_levels(qh, fh, lvl_slabs, nlev):
    ns = qh.shape[0] // SUBLANES
    kh = 1.0 - fh
    qs, ks = _slabs(qh), _slabs(kh)
    incl, tot, rexc = _slabs(fh), _slabs(fh), [None] * ns
    t8 = lax.broadcasted_iota(jnp.int32, (SUBLANES, qh.shape[1]), 0)
    zero = jnp.zeros((SUBLANES, qh.shape[1]), F32)
    qb, kb = qh.astype(BF16), kh.astype(BF16)
    diag = _slabs(_nt_dot(qb, kb))
    p = [jnp.where(lvl_slabs[r] == DIAG, diag[r], 0.0) for r in range(ns)]
    for m in range(nlev):
        h = 1 << m
        hs = h // SUBLANES
        upper = [hs == 0 or (r // hs) % 2 == 1 for r in range(ns)]
        lower = [hs == 0 or (r // hs) % 2 == 0 for r in range(ns)]
        a = jnp.concatenate([qs[r] * incl[r] if upper[r] else zero for r in range(ns)], axis=0)
        if m == 0:
            b = kb
        else:
            b = jnp.concatenate([ks[r] * rexc[r] if lower[r] else zero for r in range(ns)],
                                axis=0).astype(BF16)
        sc = _slabs(_nt_dot(a.astype(BF16), b))
        p = [jnp.where(lvl_slabs[r] == m, sc[r], p[r]) if upper[r] else p[r] for r in range(ns)]
        yield
        if hs == 0:
            up = (t8 & h) != 0
            for r in range(ns):
                below = pltpu.roll(tot[r], h, axis=0)
                above = pltpu.roll(tot[r], SUBLANES - h, axis=0)
                incl[r] = incl[r] * jnp.where(up, below, 1.0)
                grow = jnp.where(up, 1.0, above)
                rexc[r] = grow if rexc[r] is None else rexc[r] * grow
                tot[r] = tot[r] * jnp.where(up, below, above)
        else:
            for b0 in range(0, ns, 2 * hs):
                t_lo, t_hi = tot[b0], tot[b0 + hs]
                both = t_lo * t_hi
                for r in range(b0, b0 + hs):
                    rexc[r] = rexc[r] * t_hi
                    tot[r] = both
                for r in range(b0 + hs, b0 + 2 * hs):
                    incl[r] = incl[r] * t_lo
                    tot[r] = both
    return jnp.concatenate(p, axis=0), qs, ks, incl, rexc, tot


def _hgrn_finish(o, nw, og):
    return o * lax.rsqrt(jnp.mean(o * o, axis=-1, keepdims=True) + EPS) * nw * og


def _hgrn_chunk(q, f, v, og, nw, states):
    nlev = CHUNK.bit_length() - 1
    lvl_slabs = _slabs(_pair_levels(CHUNK))
    ys, new_states = [], []
    for h0 in range(0, HG_HEADS, HEAD_GROUP):
        group = range(h0, h0 + HEAD_GROUP)
        sweeps = {hd: _hgrn_levels(q[:, hd * HG_DK:(hd + 1) * HG_DK],
                                   f[:, hd * HG_DK:(hd + 1) * HG_DK], lvl_slabs, nlev)
                  for hd in group}
        swept = {}
        for _ in range(nlev + 1):
            for hd in group:
                try:
                    next(sweeps[hd])
                except StopIteration as stop:
                    swept[hd] = stop.value
            yield
        for hd in group:
            sl = slice(hd * HG_DK, (hd + 1) * HG_DK)
            p, qs, ks, incl, rexc, tot = swept[hd]
            ns = len(qs)
            qd = jnp.concatenate([qs[r] * incl[r] for r in range(ns)], axis=0).astype(BF16)
            kd = jnp.concatenate([ks[r] * rexc[r] for r in range(ns)], axis=0).astype(BF16)
            st = states[hd]
            o = (jnp.dot(p.astype(BF16), v[:, sl], preferred_element_type=F32)
                 + _nt_dot(qd, st.astype(BF16)))
            new_states.append(st * tot[0][0:1, :] + _tn_dot(v[:, sl], kd))
            ys.append(_hgrn_finish(o, nw[:, sl], og[:, sl]))
            yield
    return ys, new_states


def _hgrn_sample_kernel(hq_ref, f_ref, hv_ref, og_ref, nw_ref, s_ref, ya_ref, so_ref, *, seq_len):
    nlev = SEQ_PAD.bit_length() - 1
    t = lax.broadcasted_iota(jnp.int32, (CHUNK, HG_W), 0)
    f = jnp.where((t & (SEQ_PAD - 1)) < seq_len, f_ref[...], 1.0)
    q = hq_ref[...].astype(F32)
    v = hv_ref[...]
    vf = v.astype(F32)
    og = og_ref[...].astype(F32)
    nw = nw_ref[...]
    lvl_slabs = _slabs(_pair_levels(CHUNK))
    assert SEQ_PAD == SUBLANES
    for hd in range(HG_HEADS):
        sl = slice(hd * HG_DK, (hd + 1) * HG_DK)
        p, qs, ks, incl, rexc, tot = _drain(_hgrn_levels(q[:, sl], f[:, sl], lvl_slabs, nlev))
        o_inter = []
        for s in range(CHUNK // SEQ_PAD):
            rows = slice(s * SEQ_PAD, (s + 1) * SEQ_PAD)
            st = s_ref[s, hd]
            o_inter.append(jnp.dot(qs[s] * incl[s], st, preferred_element_type=F32))
            decay = jnp.broadcast_to(tot[s][0:1, :], (HG_DV, HG_DK)).T
            so_ref[s, hd] = st * decay + _tn_dot(ks[s] * rexc[s], vf[rows, sl])
        o = (jnp.dot(p.astype(BF16), v[:, sl], preferred_element_type=F32)
             + jnp.concatenate(o_inter, axis=0))
        ya_ref[:, sl] = _hgrn_finish(o, nw[:, sl], og[:, sl]).astype(BF16)


def _hgrn_sample(hq, f, hv, og, nw, state, seq_len):
    n = hq.shape[0]
    spc = CHUNK // SEQ_PAD
    blk = pl.BlockSpec((CHUNK, HG_W), lambda i: (i, 0))
    sblk = pl.BlockSpec((spc, HG_HEADS, HG_DK, HG_DV), lambda i: (i, 0, 0, 0))
    state_block = _nbytes((spc, HG_HEADS, HG_DK, HG_DV), F32)
    vmem = _vmem_limit(
        resident=0,
        streamed=2 * state_block + _nbytes((CHUNK, HG_W), F32) + 4 * _nbytes((CHUNK, HG_W), BF16),
        scratch=0,
        temporaries=16 * _nbytes((CHUNK, HG_W), F32))
    return pl.pallas_call(
        functools.partial(_hgrn_sample_kernel, seq_len=seq_len),
        grid=(n // CHUNK,),
        in_specs=[blk, blk, blk, blk, _const_spec((1, HG_W)), sblk],
        out_specs=[blk, sblk],
        out_shape=(jax.ShapeDtypeStruct((n, HG_W), BF16),
                   jax.ShapeDtypeStruct(state.shape, F32)),
        compiler_params=pltpu.CompilerParams(
            dimension_semantics=("arbitrary",), vmem_limit_bytes=vmem),
        name="hgrn_sample",
    )(hq, f, hv, og, nw, state)


LOG2E = 1.4426950408889634


def _alibi_slope(head):
    return LOG2E * 2.0 ** (-8.0 * (head + 1) / N_Q_HEADS)


def _tile_kv_head(x, g):
    lane = lax.broadcasted_iota(jnp.int32, x.shape, 1)
    other = pltpu.roll(x, HEAD_DIM, axis=1)
    mine = (lane >> HEAD_SHIFT) == g
    one = jnp.where(mine, x, other).astype(BF16)
    return jnp.concatenate([one] * (GROUP_W // KV_W), axis=1)


def _attn_group(qg, kt, vt, bias_of, sink_of):
    lane_head = lax.broadcasted_iota(jnp.int32, kt.shape, 1) >> HEAD_SHIFT
    zero = jnp.zeros_like(kt)
    vbd = jnp.concatenate([jnp.where(lane_head == hh, vt, zero) for hh in range(Q_PER_KV)], axis=0)
    tq = lax.broadcasted_iota(jnp.int32, (WINDOW, WINDOW), 0)
    tk = lax.broadcasted_iota(jnp.int32, (WINDOW, WINDOW), 1)
    in_cur = tk <= tq
    ps, invs = [], []
    for hh in range(Q_PER_KV):
        s2 = _nt_dot(qg, jnp.where(lane_head == hh, kt, zero))
        s = jnp.where(in_cur, s2[:, WINDOW:], s2[:, :WINDOW]) + bias_of(hh)
        sink = sink_of(hh) * LOG2E
        m = jnp.maximum(jnp.max(s, axis=-1, keepdims=True), sink)
        yield
        p = jnp.exp2(s - m)
        denom = jnp.sum(p, axis=-1, keepdims=True) + jnp.exp2(sink - m)
        ps.append(jnp.where(in_cur, 0.0, p).astype(BF16))
        ps.append(jnp.where(in_cur, p, 0.0).astype(BF16))
        invs.append(1.0 / denom)
        yield
    o = jnp.dot(jnp.concatenate(ps, axis=1), vbd, preferred_element_type=F32)
    out_head = lax.broadcasted_iota(jnp.int32, o.shape, 1) >> HEAD_SHIFT
    scale = invs[Q_PER_KV - 1]
    for hh in range(Q_PER_KV - 2, -1, -1):
        scale = jnp.where(out_head == hh, invs[hh], scale)
    yield
    return o * scale


def _init_band_bias(bias_ref):
    tq = lax.broadcasted_iota(jnp.int32, (WINDOW, WINDOW), 0)
    c = lax.broadcasted_iota(jnp.int32, (WINDOW, WINDOW), 1)
    in_cur = c <= tq
    dist = jnp.where(in_cur, tq - c, WINDOW + tq - c)
    for h in range(N_Q_HEADS):
        bias = -_alibi_slope(h) * dist.astype(F32)
        bias_ref[0, h] = bias
        bias_ref[1, h] = jnp.where(in_cur, bias, NEG_BIG)


def _swa_block(q, k2, v2, bias_ref, table, sink_ref):
    outs = []
    for g in range(N_KV_HEADS):
        kt = _tile_kv_head(k2, g)
        vt = _tile_kv_head(v2, g)
        o = yield from _attn_group(q[:, g * GROUP_W:(g + 1) * GROUP_W], kt, vt,
                                   lambda hh, g=g: bias_ref[table, g * Q_PER_KV + hh],
                                   lambda hh, g=g: sink_ref[g * Q_PER_KV + hh])
        outs.append(o)
    return outs


SWA_SAMPLE_SEQS = 16
SWA_SAMPLE_BATCH = 8
STACK = Q_PER_KV * SEQ_PAD


def _swa_sample_kernel(sink_ref, q_ref, kn_ref, vn_ref, ckt_ref, cvt_ref,
                       y_ref, ckot_ref, cvot_ref, bias_ref, sinkv_ref, *, seq_len):
    cw = ckt_ref.shape[2]

    @pl.when(pl.program_id(0) == 0)
    def _():
        rows = SWA_SAMPLE_BATCH * STACK
        r = lax.broadcasted_iota(jnp.int32, (rows, cw), 0)
        c = lax.broadcasted_iota(jnp.int32, (rows, cw), 1)
        tq = r & (SEQ_PAD - 1)
        hh = (r >> (SEQ_PAD.bit_length() - 1)) & (Q_PER_KV - 1)
        dist_c = cw + tq - c
        valid_c = (dist_c >= 0) & (dist_c < WINDOW)
        dist_n = tq - c
        valid_n = (dist_n >= 0) & (dist_n < WINDOW) & (c < seq_len)
        for g in range(N_KV_HEADS):
            slope = jnp.full((rows, cw), _alibi_slope(g * Q_PER_KV + Q_PER_KV - 1), F32)
            sink = jnp.full((rows, cw), sink_ref[g * Q_PER_KV + Q_PER_KV - 1] * LOG2E, F32)
            for h in range(Q_PER_KV - 2, -1, -1):
                slope = jnp.where(hh == h, _alibi_slope(g * Q_PER_KV + h), slope)
                sink = jnp.where(hh == h, sink_ref[g * Q_PER_KV + h] * LOG2E, sink)
            bias_ref[g, 0] = jnp.where(valid_c, -slope * dist_c.astype(F32), NEG_BIG)
            bias_ref[g, 1] = jnp.where(valid_n, -slope * dist_n.astype(F32), NEG_BIG)
            sinkv_ref[g] = sink

    lane = lax.broadcasted_iota(jnp.int32, (KV_W, cw), 1)
    keep = lane < cw - seq_len
    pad = jnp.zeros((cw - SEQ_PAD, KV_W), F32)

    nb = SWA_SAMPLE_BATCH
    shift = cw - seq_len

    def body(i, carry):
        seqs = [i * nb + j for j in range(nb)]
        q = [q_ref[s].astype(F32) for s in seqs]
        k_new = [jnp.concatenate([kn_ref[s], pad], axis=0) for s in seqs]
        v_new = [jnp.concatenate([vn_ref[s], pad], axis=0) for s in seqs]
        ckt = [ckt_ref[s] for s in seqs]
        cvt = [cvt_ref[s] for s in seqs]
        for g in range(N_KV_HEADS):
            ch = slice(g * HEAD_DIM, (g + 1) * HEAD_DIM)
            qs = [jnp.concatenate(
                [qj[:, (g * Q_PER_KV + h) * HEAD_DIM:(g * Q_PER_KV + h + 1) * HEAD_DIM]
                 for h in range(Q_PER_KV)], axis=0).astype(BF16) for qj in q]
            s_c = jnp.concatenate(
                [jnp.dot(qs[j], ckt[j][ch, :].astype(BF16), preferred_element_type=F32)
                 for j in range(nb)], axis=0) + bias_ref[g, 0]
            s_n = jnp.concatenate(
                [_nt_dot(qs[j], k_new[j][:, ch].astype(BF16)) for j in range(nb)],
                axis=0) + bias_ref[g, 1]
            sink = sinkv_ref[g][:, 0:1]
            m = jnp.maximum(jnp.maximum(jnp.max(s_c, axis=-1, keepdims=True),
                                        jnp.max(s_n, axis=-1, keepdims=True)), sink)
            p_c = jnp.exp2(s_c - m)
            p_n = jnp.exp2(s_n - m)
            denom = (jnp.sum(p_c, axis=-1, keepdims=True) + jnp.sum(p_n, axis=-1, keepdims=True)
                     + jnp.exp2(sink - m))
            inv = 1.0 / denom
            p_c, p_n = p_c.astype(BF16), p_n.astype(BF16)
            for j in range(nb):
                rows = slice(j * STACK, (j + 1) * STACK)
                o = (_nt_dot(p_c[rows, :], cvt[j][ch, :].astype(BF16))
                     + jnp.dot(p_n[rows, :], v_new[j][:, ch].astype(BF16),
                               preferred_element_type=F32)) * inv[rows, :]
                y = jnp.concatenate(
                    [o[h * SEQ_PAD:(h + 1) * SEQ_PAD, :] for h in range(Q_PER_KV)], axis=1)
                y_ref[seqs[j], :, g * GROUP_W:(g + 1) * GROUP_W] = y.astype(BF16)
        for j in range(nb):
            ckot_ref[seqs[j]] = jnp.where(keep, pltpu.roll(ckt[j], shift, axis=1),
                                          _roll_rows(k_new[j], shift).T)
            cvot_ref[seqs[j]] = jnp.where(keep, pltpu.roll(cvt[j], shift, axis=1),
                                          _roll_rows(v_new[j], shift).T)
        return carry

    lax.fori_loop(0, q_ref.shape[0] // nb, body, 0)


def _swa_sample(sinks, aq, ak, av, cache_kt, cache_vt, seq_len):
    nseq, _, cw = cache_kt.shape
    g = SWA_SAMPLE_SEQS
    blk = lambda r, w: pl.BlockSpec((g, r, w), lambda i: (i, 0, 0))
    rows = SWA_SAMPLE_BATCH * STACK
    table = _nbytes((N_KV_HEADS, rows, cw), F32)
    vmem = _vmem_limit(
        resident=0,
        streamed=(4 * _nbytes((g, KV_W, cw), F32) + 2 * _nbytes((g, SEQ_PAD, Q_W), BF16)
                  + 2 * _nbytes((g, SEQ_PAD, KV_W), F32)),
        scratch=3 * table,
        temporaries=(4 * SWA_SAMPLE_BATCH + 8) * _nbytes((KV_W, cw), F32))
    return pl.pallas_call(
        functools.partial(_swa_sample_kernel, seq_len=seq_len),
        grid=(nseq // g,),
        in_specs=[pl.BlockSpec(memory_space=pltpu.SMEM),
                  blk(SEQ_PAD, Q_W), blk(SEQ_PAD, KV_W), blk(SEQ_PAD, KV_W),
                  blk(KV_W, cw), blk(KV_W, cw)],
        out_specs=[blk(SEQ_PAD, Q_W), blk(KV_W, cw), blk(KV_W, cw)],
        out_shape=(jax.ShapeDtypeStruct((nseq, SEQ_PAD, Q_W), BF16),
                   jax.ShapeDtypeStruct((nseq, KV_W, cw), F32),
                   jax.ShapeDtypeStruct((nseq, KV_W, cw), F32)),
        scratch_shapes=[pltpu.VMEM((N_KV_HEADS, 2, rows, cw), F32),
                        pltpu.VMEM((N_KV_HEADS, rows, cw), F32)],
        compiler_params=pltpu.CompilerParams(
            dimension_semantics=("arbitrary",), vmem_limit_bytes=vmem),
        name="swa_sample",
    )(sinks, aq, ak, av, cache_kt, cache_vt)


def _merge_rows(x, ya, yb, ga, gb, woh_ref, woa_ref, wout_ref):
    branch_a = jnp.dot(ya, woh_ref[...], preferred_element_type=F32)
    yield
    branch_b = jnp.dot(yb, woa_ref[...], preferred_element_type=F32)
    yield
    merged = ga.astype(F32) * branch_a + gb.astype(F32) * branch_b
    x1 = x + jnp.dot(merged.astype(BF16), wout_ref[...], preferred_element_type=F32)
    yield
    return x1


def _merge_ffn_rows(x, ya, yb, ga, gb, woh_ref, woa_ref, wout_ref, n2_ref, wg_ref, wu_ref, wd_ref,
                    act_ref):
    x1 = yield from _merge_rows(x, ya, yb, ga, gb, woh_ref, woa_ref, wout_ref)
    return (yield from _ffn_rows(x1, n2_ref, wg_ref, wu_ref, wd_ref, act_ref))


def _ffn_rows(x1, n2_ref, wg_ref, wu_ref, wd_ref, act_ref):
    x = x1
    h2 = (x1 * lax.rsqrt(jnp.mean(x1 * x1, axis=-1, keepdims=True) + EPS) * n2_ref[...]).astype(BF16)
    d_ff = wg_ref.shape[1]
    for c in range(d_ff // FFN_CHUNK):
        cols = slice(c * FFN_CHUNK, (c + 1) * FFN_CHUNK)
        gate = jnp.dot(h2, wg_ref[:, cols], preferred_element_type=F32)
        yield
        up = jnp.dot(h2, wu_ref[:, cols], preferred_element_type=F32)
        yield
        act_ref[:, cols] = (gate * _sigmoid(gate) * up).astype(BF16)
        yield
    act = act_ref[...]
    outs = []
    for n in range(x.shape[1] // DOWN_TILE):
        cols = slice(n * DOWN_TILE, (n + 1) * DOWN_TILE)
        outs.append(x1[:, cols] + jnp.dot(act, wd_ref[:, cols], preferred_element_type=F32))
        yield
    return jnp.concatenate(outs, axis=1)


def _merge_ffn_kernel(x_ref, ya_ref, yb_ref, ga_ref, gb_ref, woh_ref, woa_ref, wout_ref,
                      n2_ref, wg_ref, wu_ref, wd_ref, o_ref, act_s):
    o_ref[...] = _drain(_merge_ffn_rows(
        x_ref[...], ya_ref[...], yb_ref[...], ga_ref[...], gb_ref[...],
        woh_ref, woa_ref, wout_ref, n2_ref, wg_ref, wu_ref, wd_ref, act_s))


def _mixers_kernel(sink_ref, hq_ref, f_ref, hv_ref, og_ref, aq_ref, kp_ref, kc_ref, vp_ref, vc_ref,
                   nw_ref, x_ref, ga_ref, gb_ref, woh_ref, woa_ref, wout_ref,
                   x1_ref, sfin_ref, klast_ref, vlast_ref, st_ref, bias_ref, *, nb):
    s = pl.program_id(0)

    @pl.when(s == 0)
    def _():
        _init_band_bias(bias_ref)
        st_ref[...] = jnp.zeros_like(st_ref)

    pos = lax.rem(s, nb)
    first = pos == 0

    ys_all, outs_all = [], []

    def recurrence():
        states = [jnp.where(first, 0.0, st_ref[hd]) for hd in range(HG_HEADS)]
        for sub in range(MIX_SUB):
            rows = slice(sub * CHUNK, (sub + 1) * CHUNK)
            ys, states = yield from _hgrn_chunk(
                hq_ref[rows, :].astype(F32), f_ref[rows, :], hv_ref[rows, :],
                og_ref[rows, :].astype(F32), nw_ref[...], states)
            ys_all.append(jnp.concatenate(ys, axis=1).astype(BF16))
        return states

    def attention():
        k_prev, v_prev = kp_ref[...], vp_ref[...]
        for sub in range(MIX_SUB):
            rows = slice(sub * CHUNK, (sub + 1) * CHUNK)
            k_cur, v_cur = kc_ref[rows, :], vc_ref[rows, :]
            table = jnp.where(first, 1, 0) if sub == 0 else 0
            outs = yield from _swa_block(
                aq_ref[rows, :], jnp.concatenate([k_prev, k_cur], axis=0),
                jnp.concatenate([v_prev, v_cur], axis=0), bias_ref, table, sink_ref)
            k_prev, v_prev = k_cur, v_cur
            outs_all.append(jnp.concatenate(outs, axis=1).astype(BF16))

    def merge():
        for pair in range(MIX_SUB // MERGE_SUB):
            ready = MERGE_SUB * (pair + 1)
            while len(ys_all) < ready or len(outs_all) < ready:
                yield WAIT
            rows = slice(pair * MERGE_SUB * CHUNK, ready * CHUNK)
            x1_ref[rows, :] = yield from _merge_rows(
                x_ref[rows, :], jnp.concatenate(ys_all[ready - MERGE_SUB:ready], axis=0),
                jnp.concatenate(outs_all[ready - MERGE_SUB:ready], axis=0),
                ga_ref[rows, :], gb_ref[rows, :], woh_ref, woa_ref, wout_ref)

    new_states, _, _ = _interleave(
        [recurrence(), attention(), merge()], [HGRN_PIECES, SWA_PIECES, MERGE_PIECES])
    for hd in range(HG_HEADS):
        st_ref[hd] = new_states[hd]

    @pl.when(pos == nb - 1)
    def _():
        for hd in range(HG_HEADS):
            sfin_ref[0, hd] = st_ref[hd].T
        last = slice((MIX_SUB - 1) * CHUNK, MIX_SUB * CHUNK)
        klast_ref[0] = kc_ref[last, :].T
        vlast_ref[0] = vc_ref[last, :].T


def _mixers(sinks, hq, f, hv, og, aq, ak, av, nw, x, ga, gb, woh, woa, wout, batch, seq):
    n, d = x.shape
    rows = MIX_SUB * CHUNK
    nb = seq // rows
    cur = lambda w: pl.BlockSpec((rows, w), lambda s: (s, 0))
    prev = lambda w: pl.BlockSpec((CHUNK, w), lambda s: (jnp.maximum(MIX_SUB * s - 1, 0), 0))
    per_seq = lambda *shape: pl.BlockSpec((1,) + shape, lambda s: (s // nb,) + (0,) * len(shape))
    state = _nbytes((HG_HEADS, HG_DV, HG_DK), F32)
    bias = _nbytes((2, N_Q_HEADS, WINDOW, WINDOW), F32)
    vmem = _vmem_limit(
        resident=_nbytes((HG_W + Q_W + d, d), BF16),
        streamed=(_nbytes((rows, HG_W), F32) + 4 * _nbytes((rows, HG_W), BF16)
                  + 2 * _nbytes((rows, d), F32) + 2 * _nbytes((rows, d), BF16)
                  + 2 * _nbytes((rows + CHUNK, KV_W), F32) + state + 2 * _nbytes((KV_W, WINDOW), F32)),
        scratch=state + bias,
        temporaries=MIX_SUB * 10 * _nbytes((CHUNK, HG_W), F32))
    return pl.pallas_call(
        functools.partial(_mixers_kernel, nb=nb),
        grid=(batch * nb,),
        in_specs=[pl.BlockSpec(memory_space=pltpu.SMEM),
                  cur(HG_W), cur(HG_W), cur(HG_W), cur(HG_W),
                  cur(Q_W), prev(KV_W), cur(KV_W), prev(KV_W), cur(KV_W),
                  _const_spec((1, HG_W)), cur(d), cur(d), cur(d),
                  _const_spec((HG_W, d)), _const_spec((Q_W, d)), _const_spec((d, d))],
        out_specs=[cur(d), per_seq(HG_HEADS, HG_DK, HG_DV),
                   per_seq(KV_W, WINDOW), per_seq(KV_W, WINDOW)],
        out_shape=(jax.ShapeDtypeStruct((n, d), F32),
                   jax.ShapeDtypeStruct((batch, HG_HEADS, HG_DK, HG_DV), F32),
                   jax.ShapeDtypeStruct((batch, KV_W, WINDOW), F32),
                   jax.ShapeDtypeStruct((batch, KV_W, WINDOW), F32)),
        scratch_shapes=[pltpu.VMEM((HG_HEADS, HG_DV, HG_DK), F32),
                        pltpu.VMEM((2, N_Q_HEADS, WINDOW, WINDOW), F32)],
        compiler_params=pltpu.CompilerParams(
            dimension_semantics=("arbitrary",), vmem_limit_bytes=vmem),
        name="mixers",
    )(sinks, hq, f, hv, og, aq, ak, ak, av, av, nw, x, ga, gb, woh, woa, wout)


def _ffn_kernel(x1_ref, n2_ref, wg_ref, wu_ref, wd_ref, o_ref, act_s):
    o_ref[...] = _drain(_ffn_rows(x1_ref[...], n2_ref, wg_ref, wu_ref, wd_ref, act_s))


def _ffn(x1, n2, wg, wu, wd):
    n, d = x1.shape
    d_ff = wg.shape[1]
    tm = FFN_ROWS
    row = pl.BlockSpec((tm, d), lambda i: (i, 0))
    vmem = _vmem_limit(
        resident=3 * _nbytes((d, d_ff), BF16),
        streamed=2 * _nbytes((tm, d), F32),
        scratch=_nbytes((tm, d_ff), BF16),
        temporaries=3 * _nbytes((tm, d), F32))
    return pl.pallas_call(
        _ffn_kernel,
        grid=(n // tm,),
        in_specs=[row, _const_spec((1, d)), _const_spec((d, d_ff)), _const_spec((d, d_ff)),
                  _const_spec((d_ff, d))],
        out_specs=row,
        out_shape=jax.ShapeDtypeStruct((n, d), F32),
        scratch_shapes=[pltpu.VMEM((tm, d_ff), BF16)],
        compiler_params=pltpu.CompilerParams(
            dimension_semantics=("arbitrary",), vmem_limit_bytes=vmem),
        name="ffn",
    )(x1, n2, wg, wu, wd)


def _merge_ffn(x, ya, yb, ga, gb, woh, woa, wout, n2, wg, wu, wd):
    n, d = x.shape
    d_ff = wg.shape[1]
    tm = ROW_TILE
    row = lambda w: pl.BlockSpec((tm, w), lambda i: (i, 0))
    vmem = _vmem_limit(
        resident=_nbytes((HG_W + Q_W + d, d), BF16) + 3 * _nbytes((d, d_ff), BF16),
        streamed=2 * _nbytes((tm, d), F32) + _nbytes((tm, HG_W + Q_W + 2 * d), BF16),
        scratch=_nbytes((tm, d_ff), BF16),
        temporaries=5 * _nbytes((tm, d), F32))
    return pl.pallas_call(
        _merge_ffn_kernel,
        grid=(n // tm,),
        in_specs=[row(d), row(HG_W), row(Q_W), row(d), row(d),
                  _const_spec((HG_W, d)), _const_spec((Q_W, d)), _const_spec((d, d)),
                  _const_spec((1, d)), _const_spec((d, d_ff)), _const_spec((d, d_ff)),
                  _const_spec((d_ff, d))],
        out_specs=row(d),
        out_shape=jax.ShapeDtypeStruct((n, d), F32),
        scratch_shapes=[pltpu.VMEM((tm, d_ff), BF16)],
        compiler_params=pltpu.CompilerParams(
            dimension_semantics=("arbitrary",), vmem_limit_bytes=vmem),
        name="merge_ffn",
    )(x, ya, yb, ga, gb, woh, woa, wout, n2, wg, wu, wd)


def kernel(x_prompt, x_sample, state_hgrn, cache_k, cache_v, norm1_w, w_in, lb_logits, hgrn_norm_w,
           q_norm_w, k_norm_w, sinks, w_o_hgrn, w_o_attn, w_out, norm2_w, w_ffn_gate, w_ffn_up,
           w_ffn_down):
    depth = w_in.shape[0]
    assert depth == 1 and lb_logits.shape[0] == 2, "single-layer step only"
    batch, seq, d = x_prompt.shape
    nseq, dec_seq, _ = x_sample.shape
    cw = cache_k.shape[2]
    assert seq % ROW_TILE == 0 and seq % CHUNK == 0 and seq >= WINDOW
    assert dec_seq <= SEQ_PAD and cw == WINDOW and w_ffn_gate.shape[2] % FFN_CHUNK == 0
    assert (nseq * SEQ_PAD) % ROW_TILE == 0 and nseq % SWA_SAMPLE_SEQS == 0

    win = w_in[0].astype(BF16)
    n1 = norm1_w[0][None, :]
    n2 = norm2_w[0][None, :]
    qn = jnp.tile(q_norm_w[0], N_Q_HEADS)[None, :] * (HEAD_DIM ** -0.5 * LOG2E)
    kn = jnp.tile(k_norm_w[0], N_KV_HEADS)[None, :]
    nw = hgrn_norm_w[0][None, :]
    head_of = jnp.arange(Q_W, dtype=jnp.int32) // HEAD_DIM
    hsel = jnp.where(head_of[:, None] == head_of[None, :], 1.0 / HEAD_DIM, 0.0).astype(BF16)
    sk = sinks[0].astype(F32)
    woh, woa, wout = w_o_hgrn[0].astype(BF16), w_o_attn[0].astype(BF16), w_out[0].astype(BF16)
    wg, wu, wd = w_ffn_gate[0].astype(BF16), w_ffn_up[0].astype(BF16), w_ffn_down[0].astype(BF16)

    xp = x_prompt.reshape(batch * seq, d)
    hq, f, hv, og, aq, ak, av, ga, gb = _proj(xp, n1, win, lb_logits, qn, kn, hsel)
    x1, s_prompt, kt_prompt, vt_prompt = _mixers(sk, hq, f, hv, og, aq, ak, av, nw, xp, ga, gb,
                                                 woh, woa, wout, batch, seq)
    y_prompt = _ffn(x1, n2, wg, wu, wd).reshape(batch, seq, d)

    def from_transposed(t):
        return t.reshape(t.shape[0], N_KV_HEADS, HEAD_DIM, t.shape[2]).transpose(0, 3, 1, 2)

    def to_transposed(c):
        return c.transpose(0, 2, 3, 1).reshape(c.shape[0], KV_W, c.shape[1])

    xs = jnp.pad(x_sample, ((0, 0), (0, SEQ_PAD - dec_seq), (0, 0))).reshape(nseq * SEQ_PAD, d)
    hq, f, hv, og, aq, ak, av, ga, gb = _proj(xs, n1, win, lb_logits, qn, kn, hsel)
    ya, s_sample = _hgrn_sample(hq, f, hv, og, nw, state_hgrn[0], dec_seq)
    yb, kt_sample, vt_sample = _swa_sample(
        sk, aq.reshape(nseq, SEQ_PAD, Q_W), ak.reshape(nseq, SEQ_PAD, KV_W),
        av.reshape(nseq, SEQ_PAD, KV_W), to_transposed(cache_k[0]), to_transposed(cache_v[0]),
        dec_seq)
    ys = _merge_ffn(xs, ya, yb.reshape(nseq * SEQ_PAD, Q_W), ga, gb, woh, woa, wout, n2, wg, wu, wd)
    y_sample = ys.reshape(nseq, SEQ_PAD, d)[:, :dec_seq]

    return (y_prompt, y_sample, s_prompt[None], from_transposed(kt_prompt)[None],
            from_transposed(vt_prompt)[None], s_sample[None], from_transposed(kt_sample)[None],
            from_transposed(vt_sample)[None])
```
